```python
import math
import jax, jax.numpy as jnp
from jax import lax
import numpy as np

D_MODEL = 2048
BATCH = 2
SEQ = 4096
DEPTH = 4

HEAD_DIM = 128
DIFF_HEADS = D_MODEL // (2 * HEAD_DIM)
ROT_DIM = HEAD_DIM // 4
ROPE_THETA = 500000.0
Q_BLOCK = 128
CONV_WIDTH = 3
FFN_DIM = 4 * D_MODEL
N_EXPERTS = 8
TOP_K = 2
EXPERT_DIM = (14336 * D_MODEL) // 4096
N_EVEN = (DEPTH + 1) // 2
N_ODD = DEPTH // 2
RMS_EPS = 1e-6

kernel_name = "hybrid_diffattn_shortconv_moe_trunk"


def rms_norm(x, gain):
    x32 = x.astype(jnp.float32)
    y = x32 * lax.rsqrt(jnp.mean(x32 * x32, axis=-1, keepdims=True) + RMS_EPS)
    return (y * gain.astype(jnp.float32)).astype(x.dtype)


def partial_rope(x, pos):
    half = ROT_DIM // 2
    inv_freq = ROPE_THETA ** (-jnp.arange(0, ROT_DIM, 2, dtype=jnp.float32) / ROT_DIM)
    ang = pos.astype(jnp.float32)[:, None] * inv_freq[None, :]
    cos = jnp.cos(ang)[None, :, None, :]
    sin = jnp.sin(ang)[None, :, None, :]
    xr = x[..., :ROT_DIM].astype(jnp.float32)
    x1, x2 = xr[..., :half], xr[..., half:]
    rot = jnp.concatenate([x1 * cos - x2 * sin, x2 * cos + x1 * sin], axis=-1)
    return jnp.concatenate([rot.astype(x.dtype), x[..., ROT_DIM:]], axis=-1)


def diff_attention(h, w_qkv, q_gain, k_gain, lq1, lk1, lq2, lk2, subln_gain, w_o, lambda_init):
    B, S, _ = h.shape
    qkv = h @ w_qkv
    q, k, v = jnp.split(qkv, 3, axis=-1)
    q = q.reshape(B, S, 2 * DIFF_HEADS, HEAD_DIM)
    k = k.reshape(B, S, 2 * DIFF_HEADS, HEAD_DIM)
    v = v.reshape(B, S, DIFF_HEADS, 2 * HEAD_DIM)
    pos = jnp.arange(S)
    q = partial_rope(rms_norm(q, q_gain), pos)
    k = partial_rope(rms_norm(k, k_gain), pos)
    lam = (jnp.exp(jnp.sum(lq1.astype(jnp.float32) * lk1.astype(jnp.float32)))
           - jnp.exp(jnp.sum(lq2.astype(jnp.float32) * lk2.astype(jnp.float32)))
           + lambda_init)
    scale = HEAD_DIM ** -0.5
    n_blocks = S // Q_BLOCK
    q_blocks = q.reshape(B, n_blocks, Q_BLOCK, 2 * DIFF_HEADS, HEAD_DIM).transpose(1, 0, 2, 3, 4)
    k_pos = jnp.arange(S)

    def one_block(args):
        qb, blk = args
        s = jnp.einsum('bqhd,bkhd->bhqk', qb, k).astype(jnp.float32) * scale
        q_pos = blk * Q_BLOCK + jnp.arange(Q_BLOCK)
        causal = k_pos[None, :] <= q_pos[:, None]
        s = jnp.where(causal[None, None], s, -jnp.inf)
        p = jax.nn.softmax(s, axis=-1).reshape(B, DIFF_HEADS, 2, Q_BLOCK, S)
        a = p[:, :, 0] - lam * p[:, :, 1]
        return jnp.einsum('bhqk,bkhe->bqhe', a.astype(v.dtype), v)

    o = lax.map(one_block, (q_blocks, jnp.arange(n_blocks)))
    o = o.transpose(1, 0, 2, 3, 4).reshape(B, S, DIFF_HEADS, 2 * HEAD_DIM)
    o = rms_norm(o, subln_gain) * (1.0 - lambda_init)
    return o.reshape(B, S, D_MODEL) @ w_o


def short_conv(h, w_in, conv_w, w_out):
    bcu = h @ w_in
    gb, gc, u = jnp.split(bcu, 3, axis=-1)
    z = gc * u
    rhs = conv_w[:, None, :].astype(z.dtype)
    zc = lax.conv_general_dilated(z, rhs, window_strides=(1,),
                                  padding=[(CONV_WIDTH - 1, 0)],
                                  dimension_numbers=('NWC', 'WIO', 'NWC'),
                                  feature_group_count=D_MODEL)
    return (gb * zc) @ w_out


def swiglu(h, w_gate, w_up, w_down):
    return (jax.nn.silu(h @ w_gate) * (h @ w_up)) @ w_down


def moe_swiglu(h, router, w_gate, w_up, w_down):
    B, S, D = h.shape
    t = h.reshape(B * S, D)
    logits = (t @ router).astype(jnp.float32)
    top_vals, top_idx = lax.top_k(logits, TOP_K)
    gates = jax.nn.softmax(top_vals, axis=-1)
    combine = jnp.sum(jax.nn.one_hot(top_idx, N_EXPERTS, dtype=jnp.float32) * gates[..., None], axis=1)
    combine = combine.astype(t.dtype)
    out = jnp.zeros_like(t)
    for e in range(N_EXPERTS):
        out = out + combine[:, e:e + 1] * swiglu(t, w_gate[e], w_up[e], w_down[e])
    return out.reshape(B, S, D)


def setup_inputs(seed: int = 0) -> dict:
    key = jax.random.key(seed)
    ks = iter(jax.random.split(key, 32))
    D, F, E, FE, dh = D_MODEL, FFN_DIM, N_EXPERTS, EXPERT_DIM, HEAD_DIM

    def w(shape, fan_in):
        return jax.random.normal(next(ks), shape, jnp.float32) * (fan_in ** -0.5)

    def gain(shape):
        return 1.0 + 0.02 * jax.random.normal(next(ks), shape, jnp.float32)

    def small(shape, s):
        return s * jax.random.normal(next(ks), shape, jnp.float32)

    return {
        "x": jax.random.normal(next(ks), (BATCH, SEQ, D), jnp.float32),
        "attn_norm": gain((N_EVEN, D)),
        "attn_w_qkv": w((N_EVEN, D, 3 * D), D),
        "attn_q_gain": gain((N_EVEN, dh)),
        "attn_k_gain": gain((N_EVEN, dh)),
        "attn_lambda_q1": small((N_EVEN, dh), 0.1),
        "attn_lambda_k1": small((N_EVEN, dh), 0.1),
        "attn_lambda_q2": small((N_EVEN, dh), 0.1),
        "attn_lambda_k2": small((N_EVEN, dh), 0.1),
        "attn_subln_gain": gain((N_EVEN, 2 * dh)),
        "attn_w_o": w((N_EVEN, D, D), D),
        "dense_norm": gain((N_EVEN, D)),
        "dense_w_gate": w((N_EVEN, D, F), D),
        "dense_w_up": w((N_EVEN, D, F), D),
        "dense_w_down": w((N_EVEN, F, D), F),
        "conv_norm": gain((N_ODD, D)),
        "conv_w_in": w((N_ODD, D, 3 * D), D),
        "conv_w": w((N_ODD, CONV_WIDTH, D), CONV_WIDTH),
        "conv_w_out": w((N_ODD, D, D), D),
        "moe_norm": gain((N_ODD, D)),
        "moe_router": w((N_ODD, D, E), D),
        "moe_w_gate": w((N_ODD, E, D, FE), D),
        "moe_w_up": w((N_ODD, E, D, FE), D),
        "moe_w_down": w((N_ODD, E, FE, D), FE),
    }


def reference(x, attn_norm, attn_w_qkv, attn_q_gain, attn_k_gain, attn_lambda_q1, attn_lambda_k1,
              attn_lambda_q2, attn_lambda_k2, attn_subln_gain, attn_w_o,
              dense_norm, dense_w_gate, dense_w_up, dense_w_down,
              conv_norm, conv_w_in, conv_w, conv_w_out,
              moe_norm, moe_router, moe_w_gate, moe_w_up, moe_w_down):
    for i in range(DEPTH):
        j = i // 2
        if i % 2 == 0:
            lambda_init = 0.8 - 0.6 * math.exp(-0.3 * i)
            x = x + diff_attention(rms_norm(x, attn_norm[j]), attn_w_qkv[j], attn_q_gain[j], attn_k_gain[j],
                                   attn_lambda_q1[j], attn_lambda_k1[j], attn_lambda_q2[j], attn_lambda_k2[j],
                                   attn_subln_gain[j], attn_w_o[j], lambda_init)
            x = x + swiglu(rms_norm(x, dense_norm[j]), dense_w_gate[j], dense_w_up[j], dense_w_down[j])
        else:
            x = x + short_conv(rms_norm(x, conv_norm[j]), conv_w_in[j], conv_w[j], conv_w_out[j])
            x = x + moe_swiglu(rms_norm(x, moe_norm[j]), moe_router[j], moe_w_gate[j], moe_w_up[j], moe_w_down[j])
    return x
```

```python
import functools
import math

import jax
import jax.numpy as jnp
from jax import lax
from jax.experimental import pallas as pl
from jax.experimental.pallas import tpu as pltpu

F32 = jnp.float32
BF16 = jnp.bfloat16

HEAD_DIM = 128
ROT_DIM = HEAD_DIM // 4
ROPE_THETA = 500000.0
RMS_EPS = 1e-6
TOP_K = 2
LANES = 128
SUBLANES = 8
VMEM_LIMIT = 56 * 1024 * 1024


def _cparams(n_axes):
    return pltpu.CompilerParams(
        dimension_semantics=("arbitrary",) * n_axes, vmem_limit_bytes=VMEM_LIMIT)


def _rms(x, gain):
    ms = jnp.mean(x * x, axis=-1, keepdims=True)
    return x * lax.rsqrt(ms + RMS_EPS) * gain


def _qkv_kernel(x_ref, g_ref, w_ref, qg_ref, kg_ref, cos_ref, sa_ref, sb_ref,
                o_ref, hn_ref, *, n_head_tiles):
    j = pl.program_id(1)

    @pl.when(j == 0)
    def _():
        hn_ref[...] = _rms(x_ref[...], g_ref[...]).astype(BF16)

    y = jnp.dot(hn_ref[...], w_ref[...], preferred_element_type=F32)
    tn = y.shape[1]

    def head_epilogue(gain_ref):
        cos, sa, sb = cos_ref[...], sa_ref[...], sb_ref[...]
        for c in range(tn // HEAD_DIM):
            yn = _rms(y[:, c * HEAD_DIM:(c + 1) * HEAD_DIM], gain_ref[...])
            rot = (yn * cos + pltpu.roll(yn, ROT_DIM // 2, 1) * sa
                   + pltpu.roll(yn, HEAD_DIM - ROT_DIM // 2, 1) * sb)
            o_ref[:, c * HEAD_DIM:(c + 1) * HEAD_DIM] = rot.astype(o_ref.dtype)

    @pl.when(j < n_head_tiles)
    def _():
        head_epilogue(qg_ref)

    @pl.when((j >= n_head_tiles) & (j < 2 * n_head_tiles))
    def _():
        head_epilogue(kg_ref)

    @pl.when(j >= 2 * n_head_tiles)
    def _():
        o_ref[...] = y.astype(o_ref.dtype)


def _rope_tables(seq):
    half = ROT_DIM // 2
    inv_freq = ROPE_THETA ** (-jnp.arange(0, ROT_DIM, 2, dtype=F32) / ROT_DIM)
    ang = jnp.arange(seq, dtype=F32)[:, None] * inv_freq[None, :]
    cos, sin = jnp.cos(ang), jnp.sin(ang)
    zeros = jnp.zeros((seq, HEAD_DIM - ROT_DIM), F32)
    zh = jnp.zeros((seq, half), F32)
    cos_t = jnp.concatenate([cos, cos, jnp.ones_like(zeros)], axis=1)
    sin_from_lower = jnp.concatenate([zh, sin, zeros], axis=1)
    sin_from_upper = jnp.concatenate([-sin, zh, zeros], axis=1)
    return cos_t, sin_from_lower, sin_from_upper


def _qkv_proj(x, gain, w, q_gain, k_gain, tables, seq, *, tm=1024, tn=512):
    t, d = x.shape
    n = w.shape[1]
    n_head_tiles = d // tn
    tab_spec = pl.BlockSpec((tm, HEAD_DIM), lambda i, j: (i % (seq // tm), 0))
    vec_spec = lambda width: pl.BlockSpec((1, width), lambda i, j: (0, 0))
    return pl.pallas_call(
        functools.partial(_qkv_kernel, n_head_tiles=n_head_tiles),
        grid=(t // tm, n // tn),
        in_specs=[pl.BlockSpec((tm, d), lambda i, j: (i, 0)), vec_spec(d),
                  pl.BlockSpec((d, tn), lambda i, j: (0, j)),
                  vec_spec(HEAD_DIM), vec_spec(HEAD_DIM), tab_spec, tab_spec, tab_spec],
        out_specs=pl.BlockSpec((tm, tn), lambda i, j: (i, j)),
        out_shape=jax.ShapeDtypeStruct((t, n), BF16),
        scratch_shapes=[pltpu.VMEM((tm, d), BF16)],
        compiler_params=_cparams(2), name="qkv_proj",
    )(x, gain.reshape(1, d), w, q_gain.reshape(1, -1), k_gain.reshape(1, -1), *tables)


def _attn_kernel(q_ref, k_ref, v_ref, lam_ref, sg_ref, o_ref, m_ref, l_ref, acc_ref,
                 *, lambda_init):
    qi, ki = pl.program_id(2), pl.program_id(3)
    tq, tk = q_ref.shape[0], k_ref.shape[0]
    scale = HEAD_DIM ** -0.5

    @pl.when(ki == 0)
    def _():
        m_ref[...] = jnp.full(m_ref.shape, -jnp.inf, F32)
        l_ref[...] = jnp.zeros(l_ref.shape, F32)
        acc_ref[...] = jnp.zeros(acc_ref.shape, F32)

    def step(masked):
        v = v_ref[...]
        for c in range(2):
            q = q_ref[:, c * HEAD_DIM:(c + 1) * HEAD_DIM]
            k = k_ref[:, c * HEAD_DIM:(c + 1) * HEAD_DIM]
            s = lax.dot_general(q, k, (((1,), (1,)), ((), ())),
                                preferred_element_type=F32) * scale
            if masked:
                row = lax.broadcasted_iota(jnp.int32, (tq, tk), 0)
                col = lax.broadcasted_iota(jnp.int32, (tq, tk), 1)
                s = jnp.where(col <= row, s, -jnp.inf)
            m_prev = m_ref[c]
            m_new = jnp.maximum(m_prev, jnp.max(s, axis=-1, keepdims=True))
            alpha = jnp.exp(m_prev - m_new)
            p = jnp.exp(s - m_new)
            l_ref[c] = alpha * l_ref[c] + jnp.sum(p, axis=-1, keepdims=True)
            acc_ref[c] = alpha * acc_ref[c] + jnp.dot(
                p.astype(BF16), v, preferred_element_type=F32)
            m_ref[c] = m_new

    @pl.when(ki < qi)
    def _():
        step(False)

    @pl.when(ki == qi)
    def _():
        step(True)
        lp = lam_ref[...]
        lam = (jnp.exp(jnp.sum(lp[0:1] * lp[1:2], axis=-1, keepdims=True))
               - jnp.exp(jnp.sum(lp[2:3] * lp[3:4], axis=-1, keepdims=True))
               + lambda_init)
        o = acc_ref[0] / l_ref[0] - lam * (acc_ref[1] / l_ref[1])
        o_ref[...] = (_rms(o, sg_ref[...]) * (1.0 - lambda_init)).astype(o_ref.dtype)


def _diff_attention(qkv, lam_params, subln_gain, batch, seq, lambda_init, *, tq=512):
    t, n3 = qkv.shape
    d = n3 // 3
    vd = 2 * HEAD_DIM
    heads = d // vd
    nq = seq // tq
    return pl.pallas_call(
        functools.partial(_attn_kernel, lambda_init=lambda_init),
        grid=(batch, heads, nq, nq),
        in_specs=[
            pl.BlockSpec((tq, vd), lambda b, h, qi, ki: (b * nq + qi, h)),
            pl.BlockSpec((tq, vd), lambda b, h, qi, ki: (b * nq + jnp.minimum(ki, qi), heads + h)),
            pl.BlockSpec((tq, vd), lambda b, h, qi, ki: (b * nq + jnp.minimum(ki, qi), 2 * heads + h)),
            pl.BlockSpec((SUBLANES, HEAD_DIM), lambda b, h, qi, ki: (0, 0)),
            pl.BlockSpec((1, vd), lambda b, h, qi, ki: (0, 0)),
        ],
        out_specs=pl.BlockSpec((tq, vd), lambda b, h, qi, ki: (b * nq + qi, h)),
        out_shape=jax.ShapeDtypeStruct((t, d), BF16),
        scratch_shapes=[pltpu.VMEM((2, tq, 1), F32), pltpu.VMEM((2, tq, 1), F32),
                        pltpu.VMEM((2, tq, vd), F32)],
        compiler_params=_cparams(4), name="diff_attention",
    )(qkv, qkv, qkv, lam_params, subln_gain.reshape(1, vd))


def _proj_residual_kernel(a_ref, w_ref, x_ref, o_ref):
    o_ref[...] = x_ref[...] + jnp.dot(a_ref[...], w_ref[...], preferred_element_type=F32)


def _proj_residual(a, w, x, *, tm=512):
    t, d = x.shape
    k = a.shape[1]
    return pl.pallas_call(
        _proj_residual_kernel,
        grid=(t // tm,),
        in_specs=[pl.BlockSpec((tm, k), lambda i: (i, 0)),
                  pl.BlockSpec((k, d), lambda i: (0, 0)),
                  pl.BlockSpec((tm, d), lambda i: (i, 0))],
        out_specs=pl.BlockSpec((tm, d), lambda i: (i, 0)),
        out_shape=jax.ShapeDtypeStruct((t, d), F32),
        compiler_params=_cparams(1), name="proj_residual",
    )(a, w, x)


def _swiglu_partial(h, wg_ref, wu_ref, wd_ref):
    gate = jnp.dot(h, wg_ref[...], preferred_element_type=F32)
    up = jnp.dot(h, wu_ref[...], preferred_element_type=F32)
    act = (gate * jax.nn.sigmoid(gate) * up).astype(BF16)
    return jnp.dot(act, wd_ref[...], preferred_element_type=F32)


def _dense_kernel(x_ref, g_ref, wg_ref, wu_ref, wd_ref, o_ref, hn_ref):
    f = pl.program_id(1)

    @pl.when(f == 0)
    def _():
        x = x_ref[...]
        hn_ref[...] = _rms(x, g_ref[...]).astype(BF16)
        o_ref[...] = x

    o_ref[...] += _swiglu_partial(hn_ref[...], wg_ref, wu_ref, wd_ref)


def _dense_swiglu(x, gain, wg, wu, wd, *, tm=512, tf=512):
    t, d = x.shape
    ff = wg.shape[1]
    return pl.pallas_call(
        _dense_kernel,
        grid=(t // tm, ff // tf),
        in_specs=[pl.BlockSpec((tm, d), lambda i, f: (i, 0)),
                  pl.BlockSpec((1, d), lambda i, f: (0, 0)),
                  pl.BlockSpec((d, tf), lambda i, f: (0, f)),
                  pl.BlockSpec((d, tf), lambda i, f: (0, f)),
                  pl.BlockSpec((tf, d), lambda i, f: (f, 0))],
        out_specs=pl.BlockSpec((tm, d), lambda i, f: (i, 0)),
        out_shape=jax.ShapeDtypeStruct((t, d), F32),
        scratch_shapes=[pltpu.VMEM((tm, d), BF16)],
        compiler_params=_cparams(2), name="dense_swiglu",
    )(x, gain.reshape(1, d), wg, wu, wd)


def _conv_in_kernel(x_ref, g_ref, wb_ref, wc_ref, wu_ref, gb_ref, z_ref, hn_ref):
    j = pl.program_id(1)

    @pl.when(j == 0)
    def _():
        hn_ref[...] = _rms(x_ref[...], g_ref[...]).astype(BF16)

    h = hn_ref[...]
    gb_ref[...] = jnp.dot(h, wb_ref[...], preferred_element_type=F32).astype(gb_ref.dtype)
    gc = jnp.dot(h, wc_ref[...], preferred_element_type=F32)
    u = jnp.dot(h, wu_ref[...], preferred_element_type=F32)
    z_ref[...] = (gc * u).astype(z_ref.dtype)


def _conv_in(x, gain, w_in, *, tm=1024, tn=512):
    t, d = x.shape
    nj = d // tn
    w_spec = lambda part: pl.BlockSpec((d, tn), lambda i, j: (0, part * nj + j))
    out_spec = pl.BlockSpec((tm, tn), lambda i, j: (i, j))
    return pl.pallas_call(
        _conv_in_kernel,
        grid=(t // tm, nj),
        in_specs=[pl.BlockSpec((tm, d), lambda i, j: (i, 0)),
                  pl.BlockSpec((1, d), lambda i, j: (0, 0)),
                  w_spec(0), w_spec(1), w_spec(2)],
        out_specs=[out_spec, out_spec],
        out_shape=[jax.ShapeDtypeStruct((t, d), BF16), jax.ShapeDtypeStruct((t, d), BF16)],
        scratch_shapes=[pltpu.VMEM((tm, d), BF16)],
        compiler_params=_cparams(2), name="conv_in",
    )(x, gain.reshape(1, d), w_in, w_in, w_in)


def _conv_out_kernel(gb_ref, z_ref, zprev_ref, cw_ref, w_ref, x_ref, o_ref, *, tiles_per_seq):
    i = pl.program_id(0)
    tm = z_ref.shape[0]
    z = z_ref[...].astype(F32)
    prev = jnp.where(i % tiles_per_seq == 0, 0.0, zprev_ref[...].astype(F32))
    row = lax.broadcasted_iota(jnp.int32, z.shape, 0)
    z1 = jnp.where(row == 0, prev[SUBLANES - 1:SUBLANES], pltpu.roll(z, 1, 0))
    z2 = pltpu.roll(z, 2, 0)
    z2 = jnp.where(row == 0, prev[SUBLANES - 2:SUBLANES - 1], z2)
    z2 = jnp.where(row == 1, prev[SUBLANES - 1:SUBLANES], z2)
    cw = cw_ref[...]
    zc = cw[0:1] * z2 + cw[1:2] * z1 + cw[2:3] * z
    a = (gb_ref[...].astype(F32) * zc).astype(BF16)
    o_ref[...] = x_ref[...] + jnp.dot(a, w_ref[...], preferred_element_type=F32)


def _conv_out(gb, z, conv_w, w_out, x, seq, *, tm=512):
    t, d = x.shape
    cw = jnp.zeros((SUBLANES, d), F32).at[:conv_w.shape[0]].set(conv_w)
    row_spec = pl.BlockSpec((tm, d), lambda i: (i, 0))
    prev_spec = pl.BlockSpec(
        (SUBLANES, d), lambda i: (jnp.maximum(i * (tm // SUBLANES) - 1, 0), 0))
    return pl.pallas_call(
        functools.partial(_conv_out_kernel, tiles_per_seq=seq // tm),
        grid=(t // tm,),
        in_specs=[row_spec, row_spec, prev_spec,
                  pl.BlockSpec((SUBLANES, d), lambda i: (0, 0)),
                  pl.BlockSpec((d, d), lambda i: (0, 0)), row_spec],
        out_specs=row_spec,
        out_shape=jax.ShapeDtypeStruct((t, d), F32),
        compiler_params=_cparams(1), name="conv_out",
    )(gb, z, z, cw, w_out, x)


def _router_kernel(x_ref, g_ref, r_ref, hn_ref, route_ref, gates_ref, counts_ref, carry_ref,
                   *, n_experts):
    i = pl.program_id(0)
    tm = x_ref.shape[0]

    @pl.when(i == 0)
    def _():
        carry_ref[...] = jnp.zeros(carry_ref.shape, F32)

    hn = _rms(x_ref[...], g_ref[...])
    hn_ref[...] = hn
    logits = jnp.dot(hn, r_ref[...], preferred_element_type=F32,
                     precision=lax.Precision.HIGHEST)
    lane = lax.broadcasted_iota(jnp.int32, logits.shape, 1)
    logits = jnp.where(lane < n_experts, logits, -jnp.inf)
    m1 = jnp.max(logits, axis=-1, keepdims=True)
    i1 = jnp.min(jnp.where(logits == m1, lane, LANES), axis=-1, keepdims=True)
    rest = jnp.where(lane == i1, -jnp.inf, logits)
    m2 = jnp.max(rest, axis=-1, keepdims=True)
    i2 = jnp.min(jnp.where(rest == m2, lane, LANES), axis=-1, keepdims=True)
    e2 = jnp.exp(m2 - m1)
    g1 = 1.0 / (1.0 + e2)
    g2 = e2 / (1.0 + e2)

    oh1 = (lane == i1).astype(F32)
    oh2 = (lane == i2).astype(F32)
    r_i = lax.broadcasted_iota(jnp.int32, (tm, tm), 0)
    c_i = lax.broadcasted_iota(jnp.int32, (tm, tm), 1)
    lower = (c_i < r_i).astype(BF16)
    before1 = jnp.dot(lower, oh1.astype(BF16), preferred_element_type=F32)
    before2 = jnp.dot(lower, oh2.astype(BF16), preferred_element_type=F32)
    tot1 = jnp.sum(oh1, axis=0, keepdims=True)
    tot2 = jnp.sum(oh2, axis=0, keepdims=True)
    carry = carry_ref[0:1]
    rank1 = jnp.sum(oh1 * (carry + before1), axis=-1, keepdims=True)
    rank2 = jnp.sum(oh2 * (carry + tot1 + before2), axis=-1, keepdims=True)
    new_carry = carry + tot1 + tot2
    carry_ref[...] = jnp.broadcast_to(new_carry, carry_ref.shape)
    counts_ref[...] = jnp.broadcast_to(new_carry, counts_ref.shape).astype(jnp.int32)

    route = jnp.where(lane == 0, i1, 0) + jnp.where(lane == 1, i2, 0)
    route += jnp.where(lane == 2, rank1.astype(jnp.int32), 0)
    route += jnp.where(lane == 3, rank2.astype(jnp.int32), 0)
    route_ref[...] = route
    gates_ref[...] = jnp.where(lane == 0, g1, 0.0) + jnp.where(lane == 1, g2, 0.0)


def _router(x, gain, router_w, *, tm=512):
    t, d = x.shape
    n_experts = router_w.shape[1]
    r_pad = jnp.zeros((d, LANES), F32).at[:, :n_experts].set(router_w)
    row_spec = pl.BlockSpec((tm, d), lambda i: (i, 0))
    lane_spec = pl.BlockSpec((tm, LANES), lambda i: (i, 0))
    return pl.pallas_call(
        functools.partial(_router_kernel, n_experts=n_experts),
        grid=(t // tm,),
        in_specs=[row_spec, pl.BlockSpec((1, d), lambda i: (0, 0)),
                  pl.BlockSpec((d, LANES), lambda i: (0, 0))],
        out_specs=[row_spec, lane_spec, lane_spec,
                   pl.BlockSpec((SUBLANES, LANES), lambda i: (0, 0))],
        out_shape=[jax.ShapeDtypeStruct((t, d), F32),
                   jax.ShapeDtypeStruct((t, LANES), jnp.int32),
                   jax.ShapeDtypeStruct((t, LANES), F32),
                   jax.ShapeDtypeStruct((SUBLANES, LANES), jnp.int32)],
        scratch_shapes=[pltpu.VMEM((SUBLANES, LANES), F32)],
        compiler_params=_cparams(1), name="moe_router",
    )(x, gain.reshape(1, d), r_pad)


def _dispatch_kernel(pos1_ref, pos2_ref, hn_ref, xs_in_ref, xs_ref, sem):
    del xs_in_ref
    tm = hn_ref.shape[0]
    base = pl.program_id(0) * tm

    def row_copy(r, pos_ref):
        return pltpu.make_async_copy(
            hn_ref.at[pl.ds(r, 1)], xs_ref.at[pl.ds(pos_ref[base + r], 1)], sem)

    def issue(r, carry):
        row_copy(r, pos1_ref).start()
        row_copy(r, pos2_ref).start()
        return carry

    def drain(r, carry):
        row_copy(r, pos1_ref).wait()
        row_copy(r, pos2_ref).wait()
        return carry

    lax.fori_loop(0, tm, issue, 0)
    lax.fori_loop(0, tm, drain, 0)


def _dispatch(hn, pos1, pos2, n_rows, *, tm=256):
    t, d = hn.shape
    xs0 = jnp.zeros((n_rows, d), hn.dtype)
    grid_spec = pltpu.PrefetchScalarGridSpec(
        num_scalar_prefetch=2, grid=(t // tm,),
        in_specs=[pl.BlockSpec((tm, d), lambda i, p1, p2: (i, 0)),
                  pl.BlockSpec(memory_space=pl.ANY)],
        out_specs=pl.BlockSpec(memory_space=pl.ANY),
        scratch_shapes=[pltpu.SemaphoreType.DMA(())])
    return pl.pallas_call(
        _dispatch_kernel, grid_spec=grid_spec,
        out_shape=jax.ShapeDtypeStruct((n_rows, d), hn.dtype),
        input_output_aliases={3: 0},
        compiler_params=_cparams(1), name="moe_dispatch",
    )(pos1, pos2, hn, xs0)


def _expert_kernel(tile_ref, te_ref, nv_ref, x_ref, wg_ref, wu_ref, wd_ref, o_ref, xb_ref):
    del tile_ref, te_ref
    i, f = pl.program_id(0), pl.program_id(1)

    @pl.when(i < nv_ref[0])
    def _():
        @pl.when(f == 0)
        def _():
            xb_ref[...] = x_ref[...].astype(BF16)
            o_ref[...] = jnp.zeros(o_ref.shape, F32)

        o_ref[...] += _swiglu_partial(xb_ref[...], wg_ref, wu_ref, wd_ref)

    @pl.when((i >= nv_ref[0]) & (f == 0))
    def _():
        o_ref[...] = jnp.zeros(o_ref.shape, F32)


def _experts(xs, tile_idx, tile_expert, n_valid, wg, wu, wd, *, tm, tf=512):
    n_rows, d = xs.shape
    fe = wg.shape[2]
    nf = fe // tf
    f_eff = lambda i, f, nv: jnp.where(i < nv[0], f, nf - 1)
    grid_spec = pltpu.PrefetchScalarGridSpec(
        num_scalar_prefetch=3, grid=(n_rows // tm, nf),
        in_specs=[
            pl.BlockSpec((tm, d), lambda i, f, ti, te, nv: (ti[i], 0)),
            pl.BlockSpec((None, d, tf), lambda i, f, ti, te, nv: (te[i], 0, f_eff(i, f, nv))),
            pl.BlockSpec((None, d, tf), lambda i, f, ti, te, nv: (te[i], 0, f_eff(i, f, nv))),
            pl.BlockSpec((None, tf, d), lambda i, f, ti, te, nv: (te[i], f_eff(i, f, nv), 0)),
        ],
        out_specs=pl.BlockSpec((tm, d), lambda i, f, ti, te, nv: (i, 0)),
        scratch_shapes=[pltpu.VMEM((tm, d), BF16)])
    return pl.pallas_call(
        _expert_kernel, grid_spec=grid_spec,
        out_shape=jax.ShapeDtypeStruct((n_rows, d), F32),
        compiler_params=_cparams(2), name="moe_experts",
    )(tile_idx, tile_expert, n_valid, xs, wg, wu, wd)


def _combine_kernel(pos1_ref, pos2_ref, x_ref, gates_ref, ys_ref, o_ref, buf_ref, sem):
    tm = x_ref.shape[0]
    base = pl.program_id(0) * tm

    def row_copy(r, k, pos_ref):
        return pltpu.make_async_copy(
            ys_ref.at[pl.ds(pos_ref[base + r], 1)], buf_ref.at[k, pl.ds(r, 1)], sem)

    def issue(r, carry):
        row_copy(r, 0, pos1_ref).start()
        row_copy(r, 1, pos2_ref).start()
        return carry

    def drain(r, carry):
        row_copy(r, 0, pos1_ref).wait()
        row_copy(r, 1, pos2_ref).wait()
        return carry

    lax.fori_loop(0, tm, issue, 0)
    lax.fori_loop(0, tm, drain, 0)
    gates = gates_ref[...]
    o_ref[...] = x_ref[...] + gates[:, 0:1] * buf_ref[0] + gates[:, 1:2] * buf_ref[1]


def _combine(x, gates, ys, pos1, pos2, *, tm=256):
    t, d = x.shape
    grid_spec = pltpu.PrefetchScalarGridSpec(
        num_scalar_prefetch=2, grid=(t // tm,),
        in_specs=[pl.BlockSpec((tm, d), lambda i, p1, p2: (i, 0)),
                  pl.BlockSpec((tm, LANES), lambda i, p1, p2: (i, 0)),
                  pl.BlockSpec(memory_space=pl.ANY)],
        out_specs=pl.BlockSpec((tm, d), lambda i, p1, p2: (i, 0)),
        scratch_shapes=[pltpu.VMEM((TOP_K, tm, d), F32), pltpu.SemaphoreType.DMA(())])
    return pl.pallas_call(
        _combine_kernel, grid_spec=grid_spec,
        out_shape=jax.ShapeDtypeStruct((t, d), F32),
        compiler_params=_cparams(1), name="moe_combine",
    )(pos1, pos2, x, gates, ys)


def _moe_layer(x, gain, router_w, wg, wu, wd, *, tm_e=512):
    t, d = x.shape
    n_experts = router_w.shape[1]
    hn, route, gates, counts = _router(x, gain, router_w)
    counts = counts[0, :n_experts]
    padded = (counts + tm_e - 1) // tm_e * tm_e
    ends = jnp.cumsum(padded)
    starts = ends - padded
    pos1 = starts[route[:, 0]] + route[:, 2]
    pos2 = starts[route[:, 1]] + route[:, 3]
    n_tiles = (t * TOP_K) // tm_e + n_experts
    n_valid = ends[-1] // tm_e
    tile_idx = jnp.minimum(jnp.arange(n_tiles, dtype=jnp.int32), n_valid - 1)
    tile_expert = jnp.sum(tile_idx[:, None] * tm_e >= ends[None, :], axis=1).astype(jnp.int32)
    xs = _dispatch(hn, pos1, pos2, n_tiles * tm_e)
    ys = _experts(xs, tile_idx, tile_expert, n_valid.reshape(1).astype(jnp.int32),
                  wg, wu, wd, tm=tm_e)
    return _combine(x, gates, ys, pos1, pos2)


def kernel(x, attn_norm, attn_w_qkv, attn_q_gain, attn_k_gain, attn_lambda_q1, attn_lambda_k1,
           attn_lambda_q2, attn_lambda_k2, attn_subln_gain, attn_w_o,
           dense_norm, dense_w_gate, dense_w_up, dense_w_down,
           conv_norm, conv_w_in, conv_w, conv_w_out,
           moe_norm, moe_router, moe_w_gate, moe_w_up, moe_w_down):
    batch, seq, d = x.shape
    depth = attn_norm.shape[0] + conv_norm.shape[0]
    tables = _rope_tables(seq)
    h = x.reshape(batch * seq, d)
    bf = lambda w: w.astype(BF16)
    for i in range(depth):
        j = i // 2
        if i % 2 == 0:
            lambda_init = 0.8 - 0.6 * math.exp(-0.3 * i)
            lam_params = jnp.zeros((SUBLANES, HEAD_DIM), F32).at[:4].set(jnp.stack(
                [attn_lambda_q1[j], attn_lambda_k1[j], attn_lambda_q2[j], attn_lambda_k2[j]]))
            qkv = _qkv_proj(h, attn_norm[j], bf(attn_w_qkv[j]), attn_q_gain[j], attn_k_gain[j],
                            tables, seq)
            a = _diff_attention(qkv, lam_params, attn_subln_gain[j], batch, seq, lambda_init)
            h = _proj_residual(a, bf(attn_w_o[j]), h)
            h = _dense_swiglu(h, dense_norm[j], bf(dense_w_gate[j]), bf(dense_w_up[j]),
                              bf(dense_w_down[j]))
        else:
            gb, z = _conv_in(h, conv_norm[j], bf(conv_w_in[j]))
            h = _conv_out(gb, z, conv_w[j], bf(conv_w_out[j]), h, seq)
            h = _moe_layer(h, moe_norm[j], moe_router[j], bf(moe_w_gate[j]), bf(moe_w_up[j]),
                           bf(moe_w_down[j]))
    return h.reshape(batch, seq, d)
```

```python
import functools
import math

import jax
import jax.numpy as jnp
from jax import lax
from jax.experimental import pallas as pl
from jax.experimental.pallas import tpu as pltpu

F32 = jnp.float32
BF16 = jnp.bfloat16

HEAD_DIM = 128
ROT_DIM = HEAD_DIM // 4
ROPE_THETA = 500000.0
RMS_EPS = 1e-6
TOP_K = 2
LANES = 128
SUBLANES = 8
MXU_COLS = 256
VMEM_LIMIT = 56 * 1024 * 1024


def _cparams(n_axes):
    return pltpu.CompilerParams(
        dimension_semantics=("arbitrary",) * n_axes, vmem_limit_bytes=VMEM_LIMIT)


def _rms(x, gain):
    ms = jnp.mean(x * x, axis=-1, keepdims=True)
    return x * lax.rsqrt(ms + RMS_EPS) * gain


def _qkv_kernel(x_ref, g_ref, w_ref, qg_ref, kg_ref, cos_ref, sa_ref, sb_ref,
                o_ref, hn_ref, *, n_head_tiles):
    j = pl.program_id(1)
    tn = o_ref.shape[1]

    @pl.when(j == 0)
    def _():
        hn_ref[...] = _rms(x_ref[...], g_ref[...]).astype(BF16)

    def head_tile(gain_ref):
        cos, sa, sb = cos_ref[...], sa_ref[...], sb_ref[...]
        for c in range(tn // MXU_COLS):
            y = jnp.dot(hn_ref[...], w_ref[:, c * MXU_COLS:(c + 1) * MXU_COLS],
                        preferred_element_type=F32)
            for hd in range(MXU_COLS // HEAD_DIM):
                yn = _rms(y[:, hd * HEAD_DIM:(hd + 1) * HEAD_DIM], gain_ref[...])
                rot = (yn * cos + pltpu.roll(yn, ROT_DIM // 2, 1) * sa
                       + pltpu.roll(yn, HEAD_DIM - ROT_DIM // 2, 1) * sb)
                col = c * MXU_COLS + hd * HEAD_DIM
                o_ref[:, col:col + HEAD_DIM] = rot.astype(o_ref.dtype)

    @pl.when(j < n_head_tiles)
    def _():
        head_tile(qg_ref)

    @pl.when((j >= n_head_tiles) & (j < 2 * n_head_tiles))
    def _():
        head_tile(kg_ref)

    @pl.when(j >= 2 * n_head_tiles)
    def _():
        o_ref[...] = jnp.dot(hn_ref[...], w_ref[...],
                             preferred_element_type=F32).astype(o_ref.dtype)


def _rope_tables(seq):
    half = ROT_DIM // 2
    inv_freq = ROPE_THETA ** (-jnp.arange(0, ROT_DIM, 2, dtype=F32) / ROT_DIM)
    ang = jnp.arange(seq, dtype=F32)[:, None] * inv_freq[None, :]
    cos, sin = jnp.cos(ang), jnp.sin(ang)
    zeros = jnp.zeros((seq, HEAD_DIM - ROT_DIM), F32)
    zh = jnp.zeros((seq, half), F32)
    cos_t = jnp.concatenate([cos, cos, jnp.ones_like(zeros)], axis=1)
    sin_from_lower = jnp.concatenate([zh, sin, zeros], axis=1)
    sin_from_upper = jnp.concatenate([-sin, zh, zeros], axis=1)
    return cos_t, sin_from_lower, sin_from_upper


def _qkv_proj(x, gain, w, layer, q_gain, k_gain, tables, seq, *, tm=1024, tn=512):
    t, d = x.shape
    n = w.shape[2]
    n_head_tiles = d // tn
    tab_spec = pl.BlockSpec((tm, HEAD_DIM), lambda i, j: (i % (seq // tm), 0))
    vec_spec = lambda width: pl.BlockSpec((1, width), lambda i, j: (0, 0))
    return pl.pallas_call(
        functools.partial(_qkv_kernel, n_head_tiles=n_head_tiles),
        grid=(t // tm, n // tn),
        in_specs=[pl.BlockSpec((tm, d), lambda i, j: (i, 0)), vec_spec(d),
                  pl.BlockSpec((None, d, tn), lambda i, j: (layer, 0, j)),
                  vec_spec(HEAD_DIM), vec_spec(HEAD_DIM), tab_spec, tab_spec, tab_spec],
        out_specs=pl.BlockSpec((tm, tn), lambda i, j: (i, j)),
        out_shape=jax.ShapeDtypeStruct((t, n), BF16),
        scratch_shapes=[pltpu.VMEM((tm, d), BF16)],
        compiler_params=_cparams(2), name="qkv_proj",
    )(x, gain.reshape(1, d), w, q_gain.reshape(1, -1), k_gain.reshape(1, -1), *tables)


def _attn_kernel(q_ref, k_ref, v_ref, lam_ref, sg_ref, o_ref, qt_ref, m_ref, l_ref, acc_ref,
                 *, lambda_init):
    qi = pl.program_id(2)
    tq = q_ref.shape[0]
    tk = tq
    to_log2 = HEAD_DIM ** -0.5 * math.log2(math.e)

    for c in range(2):
        qt_ref[c] = q_ref[:, c * HEAD_DIM:(c + 1) * HEAD_DIM].astype(F32).T.astype(BF16)
    m_ref[...] = jnp.full(m_ref.shape, -jnp.inf, F32)
    l_ref[...] = jnp.zeros(l_ref.shape, F32)
    acc_ref[...] = jnp.zeros(acc_ref.shape, F32)

    def kv_block(k0, masked):
        v = v_ref[pl.ds(k0, tk), :]
        for c in range(2):
            k = k_ref[pl.ds(k0, tk), c * HEAD_DIM:(c + 1) * HEAD_DIM]
            st = jnp.dot(k, qt_ref[c], preferred_element_type=F32) * to_log2
            if masked:
                kpos = lax.broadcasted_iota(jnp.int32, (tk, tq), 0)
                qpos = lax.broadcasted_iota(jnp.int32, (tk, tq), 1)
                st = jnp.where(kpos <= qpos, st, -jnp.inf)
            m_prev = m_ref[c]
            m_new = jnp.maximum(m_prev, jnp.max(st, axis=0, keepdims=True))
            alpha = jnp.exp2(m_prev - m_new)
            pt = jnp.exp2(st - m_new)
            l_ref[c] = alpha * l_ref[c] + jnp.sum(pt, axis=0, keepdims=True)
            pv = lax.dot_general(v, pt.astype(BF16), (((0,), (0,)), ((), ())),
                                 preferred_element_type=F32)
            acc_ref[c] = alpha * acc_ref[c] + pv
            m_ref[c] = m_new

    def body(ki, carry):
        kv_block(pl.multiple_of(ki * tk, tk), False)
        return carry

    lax.fori_loop(0, qi, body, 0)
    kv_block(pl.multiple_of(qi * tk, tk), True)

    lp = lam_ref[...]
    lam = (jnp.exp(jnp.sum(lp[0:1] * lp[1:2], axis=-1, keepdims=True))
           - jnp.exp(jnp.sum(lp[2:3] * lp[3:4], axis=-1, keepdims=True))
           + lambda_init)
    ot = acc_ref[0] * (1.0 / l_ref[0]) - lam * (acc_ref[1] * (1.0 / l_ref[1]))
    ms = jnp.mean(ot * ot, axis=0, keepdims=True)
    on = ot * lax.rsqrt(ms + RMS_EPS) * sg_ref[...] * (1.0 - lambda_init)
    o_ref[...] = on.T.astype(o_ref.dtype)


def _diff_attention(qkv, lam_params, subln_gain, batch, seq, lambda_init, *, tq=512):
    t, n3 = qkv.shape
    d = n3 // 3
    vd = 2 * HEAD_DIM
    heads = d // vd
    nq = seq // tq
    return pl.pallas_call(
        functools.partial(_attn_kernel, lambda_init=lambda_init),
        grid=(batch, heads, nq),
        in_specs=[
            pl.BlockSpec((tq, vd), lambda b, h, qi: (b * nq + qi, h)),
            pl.BlockSpec((seq, vd), lambda b, h, qi: (b, heads + h)),
            pl.BlockSpec((seq, vd), lambda b, h, qi: (b, 2 * heads + h)),
            pl.BlockSpec((SUBLANES, HEAD_DIM), lambda b, h, qi: (0, 0)),
            pl.BlockSpec((vd, 1), lambda b, h, qi: (0, 0)),
        ],
        out_specs=pl.BlockSpec((tq, vd), lambda b, h, qi: (b * nq + qi, h)),
        out_shape=jax.ShapeDtypeStruct((t, d), BF16),
        scratch_shapes=[pltpu.VMEM((2, HEAD_DIM, tq), BF16),
                        pltpu.VMEM((2, 1, tq), F32), pltpu.VMEM((2, 1, tq), F32),
                        pltpu.VMEM((2, vd, tq), F32)],
        compiler_params=_cparams(3), name="diff_attention",
    )(qkv, qkv, qkv, lam_params, subln_gain.reshape(vd, 1))


def _proj_residual_kernel(a_ref, w_ref, x_ref, o_ref):
    o_ref[...] = x_ref[...] + jnp.dot(a_ref[...], w_ref[...], preferred_element_type=F32)


def _proj_residual(a, w, layer, x, *, tm=512):
    t, d = x.shape
    k = a.shape[1]
    return pl.pallas_call(
        _proj_residual_kernel,
        grid=(t // tm,),
        in_specs=[pl.BlockSpec((tm, k), lambda i: (i, 0)),
                  pl.BlockSpec((None, k, d), lambda i: (layer, 0, 0)),
                  pl.BlockSpec((tm, d), lambda i: (i, 0))],
        out_specs=pl.BlockSpec((tm, d), lambda i: (i, 0)),
        out_shape=jax.ShapeDtypeStruct((t, d), F32),
        compiler_params=_cparams(1), name="proj_residual",
    )(a, w, x)


def _swiglu_partial(h, wg_ref, wu_ref, wd_ref):
    gate = jnp.dot(h, wg_ref[...], preferred_element_type=F32)
    up = jnp.dot(h, wu_ref[...], preferred_element_type=F32)
    act = (gate * jax.nn.sigmoid(gate) * up).astype(BF16)
    return jnp.dot(act, wd_ref[...], preferred_element_type=F32)


def _dense_kernel(x_ref, g_ref, wg_ref, wu_ref, wd_ref, o_ref, hn_ref):
    f = pl.program_id(1)

    @pl.when(f == 0)
    def _():
        x = x_ref[...]
        hn_ref[...] = _rms(x, g_ref[...]).astype(BF16)
        o_ref[...] = x

    o_ref[...] += _swiglu_partial(hn_ref[...], wg_ref, wu_ref, wd_ref)


def _dense_swiglu(x, gain, wg, wu, wd, layer, *, tm=512, tf=512):
    t, d = x.shape
    ff = wg.shape[2]
    return pl.pallas_call(
        _dense_kernel,
        grid=(t // tm, ff // tf),
        in_specs=[pl.BlockSpec((tm, d), lambda i, f: (i, 0)),
                  pl.BlockSpec((1, d), lambda i, f: (0, 0)),
                  pl.BlockSpec((None, d, tf), lambda i, f: (layer, 0, f)),
                  pl.BlockSpec((None, d, tf), lambda i, f: (layer, 0, f)),
                  pl.BlockSpec((None, tf, d), lambda i, f: (layer, f, 0))],
        out_specs=pl.BlockSpec((tm, d), lambda i, f: (i, 0)),
        out_shape=jax.ShapeDtypeStruct((t, d), F32),
        scratch_shapes=[pltpu.VMEM((tm, d), BF16)],
        compiler_params=_cparams(2), name="dense_swiglu",
    )(x, gain.reshape(1, d), wg, wu, wd)


def _conv_in_kernel(x_ref, g_ref, wb_ref, wc_ref, wu_ref, gb_ref, z_ref, hn_ref):
    j = pl.program_id(1)

    @pl.when(j == 0)
    def _():
        hn_ref[...] = _rms(x_ref[...], g_ref[...]).astype(BF16)

    h = hn_ref[...]
    gb_ref[...] = jnp.dot(h, wb_ref[...], preferred_element_type=F32).astype(gb_ref.dtype)
    gc = jnp.dot(h, wc_ref[...], preferred_element_type=F32)
    u = jnp.dot(h, wu_ref[...], preferred_element_type=F32)
    z_ref[...] = (gc * u).astype(z_ref.dtype)


def _conv_in(x, gain, w_in, layer, *, tm=1024, tn=512):
    t, d = x.shape
    nj = d // tn
    w_spec = lambda part: pl.BlockSpec((None, d, tn), lambda i, j: (layer, 0, part * nj + j))
    out_spec = pl.BlockSpec((tm, tn), lambda i, j: (i, j))
    return pl.pallas_call(
        _conv_in_kernel,
        grid=(t // tm, nj),
        in_specs=[pl.BlockSpec((tm, d), lambda i, j: (i, 0)),
                  pl.BlockSpec((1, d), lambda i, j: (0, 0)),
                  w_spec(0), w_spec(1), w_spec(2)],
        out_specs=[out_spec, out_spec],
        out_shape=[jax.ShapeDtypeStruct((t, d), BF16), jax.ShapeDtypeStruct((t, d), BF16)],
        scratch_shapes=[pltpu.VMEM((tm, d), BF16)],
        compiler_params=_cparams(2), name="conv_in",
    )(x, gain.reshape(1, d), w_in, w_in, w_in)


def _conv_out_kernel(gb_ref, z_ref, zprev_ref, cw_ref, w_ref, x_ref, o_ref, *, tiles_per_seq):
    i = pl.program_id(0)
    z = z_ref[...].astype(F32)
    prev = jnp.where(i % tiles_per_seq == 0, 0.0, zprev_ref[...].astype(F32))
    row = lax.broadcasted_iota(jnp.int32, z.shape, 0)
    z1 = jnp.where(row == 0, prev[SUBLANES - 1:SUBLANES], pltpu.roll(z, 1, 0))
    z2 = pltpu.roll(z, 2, 0)
    z2 = jnp.where(row == 0, prev[SUBLANES - 2:SUBLANES - 1], z2)
    z2 = jnp.where(row == 1, prev[SUBLANES - 1:SUBLANES], z2)
    cw = cw_ref[...]
    zc = cw[0:1] * z2 + cw[1:2] * z1 + cw[2:3] * z
    a = (gb_ref[...].astype(F32) * zc).astype(BF16)
    o_ref[...] = x_ref[...] + jnp.dot(a, w_ref[...], preferred_element_type=F32)


def _conv_out(gb, z, conv_w, w_out, layer, x, seq, *, tm=512):
    t, d = x.shape
    cw = jnp.zeros((SUBLANES, d), F32).at[:conv_w.shape[0]].set(conv_w)
    row_spec = pl.BlockSpec((tm, d), lambda i: (i, 0))
    prev_spec = pl.BlockSpec(
        (SUBLANES, d), lambda i: (jnp.maximum(i * (tm // SUBLANES) - 1, 0), 0))
    return pl.pallas_call(
        functools.partial(_conv_out_kernel, tiles_per_seq=seq // tm),
        grid=(t // tm,),
        in_specs=[row_spec, row_spec, prev_spec,
                  pl.BlockSpec((SUBLANES, d), lambda i: (0, 0)),
                  pl.BlockSpec((None, d, d), lambda i: (layer, 0, 0)), row_spec],
        out_specs=row_spec,
        out_shape=jax.ShapeDtypeStruct((t, d), F32),
        compiler_params=_cparams(1), name="conv_out",
    )(gb, z, z, cw, w_out, x)


def _router_kernel(x_ref, g_ref, r_ref, hn_ref, route_ref, gates_ref, counts_ref, carry_ref,
                   *, n_experts):
    i = pl.program_id(0)
    tm = x_ref.shape[0]

    @pl.when(i == 0)
    def _():
        carry_ref[...] = jnp.zeros(carry_ref.shape, F32)

    hn = _rms(x_ref[...], g_ref[...])
    hn_ref[...] = hn
    logits = jnp.dot(hn, r_ref[...], preferred_element_type=F32,
                     precision=lax.Precision.HIGHEST)
    lane = lax.broadcasted_iota(jnp.int32, logits.shape, 1)
    logits = jnp.where(lane < n_experts, logits, -jnp.inf)
    m1 = jnp.max(logits, axis=-1, keepdims=True)
    i1 = jnp.min(jnp.where(logits == m1, lane, LANES), axis=-1, keepdims=True)
    rest = jnp.where(lane == i1, -jnp.inf, logits)
    m2 = jnp.max(rest, axis=-1, keepdims=True)
    i2 = jnp.min(jnp.where(rest == m2, lane, LANES), axis=-1, keepdims=True)
    e2 = jnp.exp(m2 - m1)
    g1 = 1.0 / (1.0 + e2)
    g2 = e2 / (1.0 + e2)

    oh1 = (lane == i1).astype(F32)
    oh2 = (lane == i2).astype(F32)
    r_i = lax.broadcasted_iota(jnp.int32, (tm, tm), 0)
    c_i = lax.broadcasted_iota(jnp.int32, (tm, tm), 1)
    lower = (c_i < r_i).astype(BF16)
    before1 = jnp.dot(lower, oh1.astype(BF16), preferred_element_type=F32)
    before2 = jnp.dot(lower, oh2.astype(BF16), preferred_element_type=F32)
    tot1 = jnp.sum(oh1, axis=0, keepdims=True)
    tot2 = jnp.sum(oh2, axis=0, keepdims=True)
    carry = carry_ref[0:1]
    rank1 = jnp.sum(oh1 * (carry + before1), axis=-1, keepdims=True)
    rank2 = jnp.sum(oh2 * (carry + tot1 + before2), axis=-1, keepdims=True)
    new_carry = carry + tot1 + tot2
    carry_ref[...] = jnp.broadcast_to(new_carry, carry_ref.shape)
    counts_ref[...] = jnp.broadcast_to(new_carry, counts_ref.shape).astype(jnp.int32)

    route = jnp.where(lane == 0, i1, 0) + jnp.where(lane == 1, i2, 0)
    route += jnp.where(lane == 2, rank1.astype(jnp.int32), 0)
    route += jnp.where(lane == 3, rank2.astype(jnp.int32), 0)
    route_ref[...] = route
    gates_ref[...] = jnp.where(lane == 0, g1, 0.0) + jnp.where(lane == 1, g2, 0.0)


def _router(x, gain, router_w, *, tm=512):
    t, d = x.shape
    n_experts = router_w.shape[1]
    r_pad = jnp.zeros((d, LANES), F32).at[:, :n_experts].set(router_w)
    row_spec = pl.BlockSpec((tm, d), lambda i: (i, 0))
    lane_spec = pl.BlockSpec((tm, LANES), lambda i: (i, 0))
    return pl.pallas_call(
        functools.partial(_router_kernel, n_experts=n_experts),
        grid=(t // tm,),
        in_specs=[row_spec, pl.BlockSpec((1, d), lambda i: (0, 0)),
                  pl.BlockSpec((d, LANES), lambda i: (0, 0))],
        out_specs=[row_spec, lane_spec, lane_spec,
                   pl.BlockSpec((SUBLANES, LANES), lambda i: (0, 0))],
        out_shape=[jax.ShapeDtypeStruct((t, d), F32),
                   jax.ShapeDtypeStruct((t, LANES), jnp.int32),
                   jax.ShapeDtypeStruct((t, LANES), F32),
                   jax.ShapeDtypeStruct((SUBLANES, LANES), jnp.int32)],
        scratch_shapes=[pltpu.VMEM((SUBLANES, LANES), F32)],
        compiler_params=_cparams(1), name="moe_router",
    )(x, gain.reshape(1, d), r_pad)


def _dispatch_kernel(pos1_ref, pos2_ref, hn_ref, xs_in_ref, xs_ref, sem):
    del xs_in_ref
    tm = hn_ref.shape[0]
    base = pl.program_id(0) * tm

    def row_copy(r, pos_ref):
        return pltpu.make_async_copy(
            hn_ref.at[pl.ds(r, 1)], xs_ref.at[pl.ds(pos_ref[base + r], 1)], sem)

    def issue(r, carry):
        row_copy(r, pos1_ref).start()
        row_copy(r, pos2_ref).start()
        return carry

    def drain(r, carry):
        row_copy(r, pos1_ref).wait()
        row_copy(r, pos2_ref).wait()
        return carry

    lax.fori_loop(0, tm, issue, 0)
    lax.fori_loop(0, tm, drain, 0)


def _dispatch(hn, pos1, pos2, n_rows, *, tm=256):
    t, d = hn.shape
    xs0 = jnp.zeros((n_rows, d), hn.dtype)
    grid_spec = pltpu.PrefetchScalarGridSpec(
        num_scalar_prefetch=2, grid=(t // tm,),
        in_specs=[pl.BlockSpec((tm, d), lambda i, p1, p2: (i, 0)),
                  pl.BlockSpec(memory_space=pl.ANY)],
        out_specs=pl.BlockSpec(memory_space=pl.ANY),
        scratch_shapes=[pltpu.SemaphoreType.DMA(())])
    return pl.pallas_call(
        _dispatch_kernel, grid_spec=grid_spec,
        out_shape=jax.ShapeDtypeStruct((n_rows, d), hn.dtype),
        input_output_aliases={3: 0},
        compiler_params=_cparams(1), name="moe_dispatch",
    )(pos1, pos2, hn, xs0)


def _expert_kernel(tile_ref, te_ref, nv_ref, x_ref, wg_ref, wu_ref, wd_ref, o_ref, xb_ref):
    del tile_ref, te_ref
    i, f = pl.program_id(0), pl.program_id(1)

    @pl.when(i < nv_ref[0])
    def _():
        @pl.when(f == 0)
        def _():
            xb_ref[...] = x_ref[...].astype(BF16)
            o_ref[...] = jnp.zeros(o_ref.shape, F32)

        o_ref[...] += _swiglu_partial(xb_ref[...], wg_ref, wu_ref, wd_ref)

    @pl.when((i >= nv_ref[0]) & (f == 0))
    def _():
        o_ref[...] = jnp.zeros(o_ref.shape, F32)


def _experts(xs, tile_idx, tile_expert, n_valid, wg, wu, wd, layer, *, tm, tf=512):
    n_rows, d = xs.shape
    fe = wg.shape[3]
    nf = fe // tf
    f_eff = lambda i, f, nv: jnp.where(i < nv[0], f, nf - 1)
    grid_spec = pltpu.PrefetchScalarGridSpec(
        num_scalar_prefetch=3, grid=(n_rows // tm, nf),
        in_specs=[
            pl.BlockSpec((tm, d), lambda i, f, ti, te, nv: (ti[i], 0)),
            pl.BlockSpec((None, None, d, tf),
                         lambda i, f, ti, te, nv: (layer, te[i], 0, f_eff(i, f, nv))),
            pl.BlockSpec((None, None, d, tf),
                         lambda i, f, ti, te, nv: (layer, te[i], 0, f_eff(i, f, nv))),
            pl.BlockSpec((None, None, tf, d),
                         lambda i, f, ti, te, nv: (layer, te[i], f_eff(i, f, nv), 0)),
        ],
        out_specs=pl.BlockSpec((tm, d), lambda i, f, ti, te, nv: (i, 0)),
        scratch_shapes=[pltpu.VMEM((tm, d), BF16)])
    return pl.pallas_call(
        _expert_kernel, grid_spec=grid_spec,
        out_shape=jax.ShapeDtypeStruct((n_rows, d), F32),
        compiler_params=_cparams(2), name="moe_experts",
    )(tile_idx, tile_expert, n_valid, xs, wg, wu, wd)


def _combine_kernel(pos1_ref, pos2_ref, x_ref, gates_ref, ys_ref, o_ref, buf_ref, sem):
    tm = x_ref.shape[0]
    base = pl.program_id(0) * tm

    def row_copy(r, k, pos_ref):
        return pltpu.make_async_copy(
            ys_ref.at[pl.ds(pos_ref[base + r], 1)], buf_ref.at[k, pl.ds(r, 1)], sem)

    def issue(r, carry):
        row_copy(r, 0, pos1_ref).start()
        row_copy(r, 1, pos2_ref).start()
        return carry

    def drain(r, carry):
        row_copy(r, 0, pos1_ref).wait()
        row_copy(r, 1, pos2_ref).wait()
        return carry

    lax.fori_loop(0, tm, issue, 0)
    lax.fori_loop(0, tm, drain, 0)
    gates = gates_ref[...]
    o_ref[...] = x_ref[...] + gates[:, 0:1] * buf_ref[0] + gates[:, 1:2] * buf_ref[1]


def _combine(x, gates, ys, pos1, pos2, *, tm=256):
    t, d = x.shape
    grid_spec = pltpu.PrefetchScalarGridSpec(
        num_scalar_prefetch=2, grid=(t // tm,),
        in_specs=[pl.BlockSpec((tm, d), lambda i, p1, p2: (i, 0)),
                  pl.BlockSpec((tm, LANES), lambda i, p1, p2: (i, 0)),
                  pl.BlockSpec(memory_space=pl.ANY)],
        out_specs=pl.BlockSpec((tm, d), lambda i, p1, p2: (i, 0)),
        scratch_shapes=[pltpu.VMEM((TOP_K, tm, d), F32), pltpu.SemaphoreType.DMA(())])
    return pl.pallas_call(
        _combine_kernel, grid_spec=grid_spec,
        out_shape=jax.ShapeDtypeStruct((t, d), F32),
        compiler_params=_cparams(1), name="moe_combine",
    )(pos1, pos2, x, gates, ys)


def _moe_layer(x, gain, router_w, wg, wu, wd, layer, *, tm_e=512):
    t, d = x.shape
    n_experts = router_w.shape[1]
    hn, route, gates, counts = _router(x, gain, router_w)
    counts = counts[0, :n_experts]
    padded = (counts + tm_e - 1) // tm_e * tm_e
    ends = jnp.cumsum(padded)
    starts = ends - padded
    pos1 = starts[route[:, 0]] + route[:, 2]
    pos2 = starts[route[:, 1]] + route[:, 3]
    n_tiles = (t * TOP_K) // tm_e + n_experts
    n_valid = ends[-1] // tm_e
    tile_idx = jnp.minimum(jnp.arange(n_tiles, dtype=jnp.int32), n_valid - 1)
    tile_expert = jnp.sum(tile_idx[:, None] * tm_e >= ends[None, :], axis=1).astype(jnp.int32)
    xs = _dispatch(hn, pos1, pos2, n_tiles * tm_e)
    ys = _experts(xs, tile_idx, tile_expert, n_valid.reshape(1).astype(jnp.int32),
                  wg, wu, wd, layer, tm=tm_e)
    return _combine(x, gates, ys, pos1, pos2)


def kernel(x, attn_norm, attn_w_qkv, attn_q_gain, attn_k_gain, attn_lambda_q1, attn_lambda_k1,
           attn_lambda_q2, attn_lambda_k2, attn_subln_gain, attn_w_o,
           dense_norm, dense_w_gate, dense_w_up, dense_w_down,
           conv_norm, conv_w_in, conv_w, conv_w_out,
           moe_norm, moe_router, moe_w_gate, moe_w_up, moe_w_down):
    batch, seq, d = x.shape
    depth = attn_norm.shape[0] + conv_norm.shape[0]
    tables = _rope_tables(seq)
    h = x.reshape(batch * seq, d)
    bf = lambda w: w.astype(BF16)
    w_qkv, w_o = bf(attn_w_qkv), bf(attn_w_o)
    w_gate, w_up, w_down = bf(dense_w_gate), bf(dense_w_up), bf(dense_w_down)
    w_in, w_out = bf(conv_w_in), bf(conv_w_out)
    e_gate, e_up, e_down = bf(moe_w_gate), bf(moe_w_up), bf(moe_w_down)
    for i in range(depth):
        j = i // 2
        if i % 2 == 0:
            lambda_init = 0.8 - 0.6 * math.exp(-0.3 * i)
            lam_params = jnp.zeros((SUBLANES, HEAD_DIM), F32).at[:4].set(jnp.stack(
                [attn_lambda_q1[j], attn_lambda_k1[j], attn_lambda_q2[j], attn_lambda_k2[j]]))
            qkv = _qkv_proj(h, attn_norm[j], w_qkv, j, attn_q_gain[j], attn_k_gain[j],
                            tables, seq)
            a = _diff_attention(qkv, lam_params, attn_subln_gain[j], batch, seq, lambda_init)
            h = _proj_residual(a, w_o, j, h)
            h = _dense_swiglu(h, dense_norm[j], w_gate, w_up, w_down, j)
        else:
            gb, z = _conv_in(h, conv_norm[j], w_in, j)
            h = _conv_out(gb, z, conv_w[j], w_out, j, h, seq)
            h = _moe_layer(h, moe_norm[j], moe_router[j], e_gate, e_up, e_down, j)
    return h.reshape(batch, seq, d)
```

```python
import functools
import math

import jax
import jax.numpy as jnp
from jax import lax
from jax.experimental import pallas as pl
from jax.experimental.pallas import tpu as pltpu

F32 = jnp.float32
BF16 = jnp.bfloat16

HEAD_DIM = 128
ROT_DIM = HEAD_DIM // 4
ROPE_THETA = 500000.0
RMS_EPS = 1e-6
TOP_K = 2
LANES = 128
SUBLANES = 8
MXU_COLS = 256
VMEM_LIMIT = 56 * 1024 * 1024


def _cparams(n_axes):
    return pltpu.CompilerParams(
        dimension_semantics=("arbitrary",) * n_axes, vmem_limit_bytes=VMEM_LIMIT)


def _rms(x, gain):
    ms = jnp.mean(x * x, axis=-1, keepdims=True)
    return x * lax.rsqrt(ms + RMS_EPS) * gain


def _qkv_kernel(x_ref, g_ref, w_ref, o_ref, hn_ref):
    @pl.when(pl.program_id(1) == 0)
    def _():
        hn_ref[...] = _rms(x_ref[...], g_ref[...]).astype(BF16)

    o_ref[...] = jnp.dot(hn_ref[...], w_ref[...], preferred_element_type=F32).astype(o_ref.dtype)


def _qkv_proj(x, gain, w, layer, *, tm=1024, tn=512):
    t, d = x.shape
    n = w.shape[2]
    return pl.pallas_call(
        _qkv_kernel,
        grid=(t // tm, n // tn),
        in_specs=[pl.BlockSpec((tm, d), lambda i, j: (i, 0)),
                  pl.BlockSpec((1, d), lambda i, j: (0, 0)),
                  pl.BlockSpec((None, d, tn), lambda i, j: (layer, 0, j))],
        out_specs=pl.BlockSpec((tm, tn), lambda i, j: (i, j)),
        out_shape=jax.ShapeDtypeStruct((t, n), BF16),
        scratch_shapes=[pltpu.VMEM((tm, d), BF16)],
        compiler_params=_cparams(2), name="qkv_proj",
    )(x, gain.reshape(1, d), w)


def _rope_tables(seq):
    inv_freq = ROPE_THETA ** (-jnp.arange(0, ROT_DIM, 2, dtype=F32) / ROT_DIM)
    ang = inv_freq[:, None] * jnp.arange(seq, dtype=F32)[None, :]
    return jnp.cos(ang), jnp.sin(ang)


def _qk_norm_rope_t(x, gain_col, cos, sin):
    half = ROT_DIM // 2
    xt = x.astype(F32).T
    ms = jnp.mean(xt * xt, axis=0, keepdims=True)
    xn = xt * lax.rsqrt(ms + RMS_EPS) * gain_col
    x1, x2 = xn[0:half], xn[half:ROT_DIM]
    rot = jnp.concatenate([x1 * cos - x2 * sin, x2 * cos + x1 * sin, xn[ROT_DIM:]], axis=0)
    return rot.astype(BF16)


def _attn_kernel(q_ref, k_ref, v_ref, lam_ref, sg_ref, qg_ref, kg_ref, cos_ref, sin_ref,
                 o_ref, kt_ref, qt_ref, m_ref, l_ref, acc_ref, sa_ref, sb_ref, mba_ref, mbb_ref,
                 *, lambda_init):
    qi = pl.program_id(2)
    tq = q_ref.shape[0]
    tk = tq
    seq = k_ref.shape[0]
    to_log2 = HEAD_DIM ** -0.5 * math.log2(math.e)

    @pl.when(qi == 0)
    def _():
        def chunk(j, carry):
            r0 = pl.multiple_of(j * tk, tk)
            cos, sin = cos_ref[:, pl.ds(r0, tk)], sin_ref[:, pl.ds(r0, tk)]
            for c in range(2):
                kt_ref[c, :, pl.ds(r0, tk)] = _qk_norm_rope_t(
                    k_ref[pl.ds(r0, tk), c * HEAD_DIM:(c + 1) * HEAD_DIM], kg_ref[...], cos, sin)
            return carry
        lax.fori_loop(0, seq // tk, chunk, 0)

    q0 = pl.multiple_of(qi * tq, tq)
    cos_q, sin_q = cos_ref[:, pl.ds(q0, tq)], sin_ref[:, pl.ds(q0, tq)]
    for c in range(2):
        qt_ref[c] = _qk_norm_rope_t(
            q_ref[:, c * HEAD_DIM:(c + 1) * HEAD_DIM], qg_ref[...], cos_q, sin_q)
    m_ref[...] = jnp.full(m_ref.shape, -jnp.inf, F32)
    l_ref[...] = jnp.zeros(l_ref.shape, F32)
    acc_ref[...] = jnp.zeros(acc_ref.shape, F32)

    def scores(k0, s_ref, mb_ref):
        for c in range(2):
            st = lax.dot_general(kt_ref[c, :, pl.ds(k0, tk)], qt_ref[c],
                                 (((0,), (0,)), ((), ())),
                                 preferred_element_type=F32) * to_log2
            s_ref[c] = st
            mb_ref[c] = jnp.max(st, axis=0, keepdims=True)

    def consume(k0, s_ref, mb_ref, masked):
        v = v_ref[pl.ds(k0, tk), :]
        for c in range(2):
            st = s_ref[c]
            if masked:
                kpos = lax.broadcasted_iota(jnp.int32, (tk, tq), 0)
                qpos = lax.broadcasted_iota(jnp.int32, (tk, tq), 1)
                st = jnp.where(kpos <= qpos, st, -jnp.inf)
                m_blk = jnp.max(st, axis=0, keepdims=True)
            else:
                m_blk = mb_ref[c]
            m_prev = m_ref[c]
            m_new = jnp.maximum(m_prev, m_blk)
            alpha = jnp.exp2(m_prev - m_new)
            pt = jnp.exp2(st - m_new)
            l_ref[c] = alpha * l_ref[c] + jnp.sum(pt, axis=0, keepdims=True)
            pv = lax.dot_general(v, pt.astype(BF16), (((0,), (0,)), ((), ())),
                                 preferred_element_type=F32)
            acc_ref[c] = alpha * acc_ref[c] + pv
            m_ref[c] = m_new

    scores(0, sa_ref, mba_ref)

    def pair(p, carry):
        k0 = pl.multiple_of(2 * p * tk, tk)
        scores(k0 + tk, sb_ref, mbb_ref)
        consume(k0, sa_ref, mba_ref, False)
        scores(k0 + 2 * tk, sa_ref, mba_ref)
        consume(k0 + tk, sb_ref, mbb_ref, False)
        return carry

    lax.fori_loop(0, qi // 2, pair, 0)

    @pl.when(qi % 2 == 0)
    def _():
        consume(q0, sa_ref, mba_ref, True)

    @pl.when(qi % 2 == 1)
    def _():
        scores(q0, sb_ref, mbb_ref)
        consume(q0 - tk, sa_ref, mba_ref, False)
        consume(q0, sb_ref, mbb_ref, True)

    lp = lam_ref[...]
    lam = (jnp.exp(jnp.sum(lp[0:1] * lp[1:2], axis=-1, keepdims=True))
           - jnp.exp(jnp.sum(lp[2:3] * lp[3:4], axis=-1, keepdims=True))
           + lambda_init)
    ot = acc_ref[0] * (1.0 / l_ref[0]) - lam * (acc_ref[1] * (1.0 / l_ref[1]))
    ms = jnp.mean(ot * ot, axis=0, keepdims=True)
    on = ot * lax.rsqrt(ms + RMS_EPS) * sg_ref[...] * (1.0 - lambda_init)
    o_ref[...] = on.T.astype(o_ref.dtype)


def _diff_attention(qkv, lam_params, subln_gain, q_gain, k_gain, tables, batch, seq,
                    lambda_init, *, tq=512):
    t, n3 = qkv.shape
    d = n3 // 3
    vd = 2 * HEAD_DIM
    heads = d // vd
    nq = seq // tq
    const = lambda shape: pl.BlockSpec(shape, lambda b, h, qi: (0, 0))
    return pl.pallas_call(
        functools.partial(_attn_kernel, lambda_init=lambda_init),
        grid=(batch, heads, nq),
        in_specs=[
            pl.BlockSpec((tq, vd), lambda b, h, qi: (b * nq + qi, h)),
            pl.BlockSpec((seq, vd), lambda b, h, qi: (b, heads + h)),
            pl.BlockSpec((seq, vd), lambda b, h, qi: (b, 2 * heads + h)),
            const((SUBLANES, HEAD_DIM)), const((vd, 1)),
            const((HEAD_DIM, 1)), const((HEAD_DIM, 1)),
            const((ROT_DIM // 2, seq)), const((ROT_DIM // 2, seq)),
        ],
        out_specs=pl.BlockSpec((tq, vd), lambda b, h, qi: (b * nq + qi, h)),
        out_shape=jax.ShapeDtypeStruct((t, d), BF16),
        scratch_shapes=[pltpu.VMEM((2, HEAD_DIM, seq), BF16),
                        pltpu.VMEM((2, HEAD_DIM, tq), BF16),
                        pltpu.VMEM((2, 1, tq), F32), pltpu.VMEM((2, 1, tq), F32),
                        pltpu.VMEM((2, vd, tq), F32),
                        pltpu.VMEM((2, tq, tq), F32), pltpu.VMEM((2, tq, tq), F32),
                        pltpu.VMEM((2, 1, tq), F32), pltpu.VMEM((2, 1, tq), F32)],
        compiler_params=_cparams(3), name="diff_attention",
    )(qkv, qkv, qkv, lam_params, subln_gain.reshape(vd, 1),
      q_gain.reshape(HEAD_DIM, 1), k_gain.reshape(HEAD_DIM, 1), *tables)


def _proj_residual_kernel(a_ref, w_ref, x_ref, o_ref):
    o_ref[...] = x_ref[...] + jnp.dot(a_ref[...], w_ref[...], preferred_element_type=F32)


def _proj_residual(a, w, layer, x, *, tm=512):
    t, d = x.shape
    k = a.shape[1]
    return pl.pallas_call(
        _proj_residual_kernel,
        grid=(t // tm,),
        in_specs=[pl.BlockSpec((tm, k), lambda i: (i, 0)),
                  pl.BlockSpec((None, k, d), lambda i: (layer, 0, 0)),
                  pl.BlockSpec((tm, d), lambda i: (i, 0))],
        out_specs=pl.BlockSpec((tm, d), lambda i: (i, 0)),
        out_shape=jax.ShapeDtypeStruct((t, d), F32),
        compiler_params=_cparams(1), name="proj_residual",
    )(a, w, x)


def _swiglu_partial(h, wg_ref, wu_ref, wd_ref):
    gate = jnp.dot(h, wg_ref[...], preferred_element_type=F32)
    up = jnp.dot(h, wu_ref[...], preferred_element_type=F32)
    act = (gate * jax.nn.sigmoid(gate) * up).astype(BF16)
    return jnp.dot(act, wd_ref[...], preferred_element_type=F32)


def _dense_kernel(x_ref, g_ref, wg_ref, wu_ref, wd_ref, o_ref, hn_ref):
    f = pl.program_id(1)

    @pl.when(f == 0)
    def _():
        x = x_ref[...]
        hn_ref[...] = _rms(x, g_ref[...]).astype(BF16)
        o_ref[...] = x

    o_ref[...] += _swiglu_partial(hn_ref[...], wg_ref, wu_ref, wd_ref)


def _dense_swiglu(x, gain, wg, wu, wd, layer, *, tm=512, tf=512):
    t, d = x.shape
    ff = wg.shape[2]
    return pl.pallas_call(
        _dense_kernel,
        grid=(t // tm, ff // tf),
        in_specs=[pl.BlockSpec((tm, d), lambda i, f: (i, 0)),
                  pl.BlockSpec((1, d), lambda i, f: (0, 0)),
                  pl.BlockSpec((None, d, tf), lambda i, f: (layer, 0, f)),
                  pl.BlockSpec((None, d, tf), lambda i, f: (layer, 0, f)),
                  pl.BlockSpec((None, tf, d), lambda i, f: (layer, f, 0))],
        out_specs=pl.BlockSpec((tm, d), lambda i, f: (i, 0)),
        out_shape=jax.ShapeDtypeStruct((t, d), F32),
        scratch_shapes=[pltpu.VMEM((tm, d), BF16)],
        compiler_params=_cparams(2), name="dense_swiglu",
    )(x, gain.reshape(1, d), wg, wu, wd)


def _conv_in_kernel(x_ref, g_ref, wb_ref, wc_ref, wu_ref, gb_ref, z_ref, hn_ref):
    j = pl.program_id(1)

    @pl.when(j == 0)
    def _():
        hn_ref[...] = _rms(x_ref[...], g_ref[...]).astype(BF16)

    h = hn_ref[...]
    gb_ref[...] = jnp.dot(h, wb_ref[...], preferred_element_type=F32).astype(gb_ref.dtype)
    gc = jnp.dot(h, wc_ref[...], preferred_element_type=F32)
    u = jnp.dot(h, wu_ref[...], preferred_element_type=F32)
    z_ref[...] = (gc * u).astype(z_ref.dtype)


def _conv_in(x, gain, w_in, layer, *, tm=1024, tn=512):
    t, d = x.shape
    nj = d // tn
    w_spec = lambda part: pl.BlockSpec((None, d, tn), lambda i, j: (layer, 0, part * nj + j))
    out_spec = pl.BlockSpec((tm, tn), lambda i, j: (i, j))
    return pl.pallas_call(
        _conv_in_kernel,
        grid=(t // tm, nj),
        in_specs=[pl.BlockSpec((tm, d), lambda i, j: (i, 0)),
                  pl.BlockSpec((1, d), lambda i, j: (0, 0)),
                  w_spec(0), w_spec(1), w_spec(2)],
        out_specs=[out_spec, out_spec],
        out_shape=[jax.ShapeDtypeStruct((t, d), BF16), jax.ShapeDtypeStruct((t, d), BF16)],
        scratch_shapes=[pltpu.VMEM((tm, d), BF16)],
        compiler_params=_cparams(2), name="conv_in",
    )(x, gain.reshape(1, d), w_in, w_in, w_in)


def _conv_out_kernel(gb_ref, z_ref, zprev_ref, cw_ref, w_ref, x_ref, o_ref, *, tiles_per_seq):
    i = pl.program_id(0)
    z = z_ref[...].astype(F32)
    prev = jnp.where(i % tiles_per_seq == 0, 0.0, zprev_ref[...].astype(F32))
    row = lax.broadcasted_iota(jnp.int32, z.shape, 0)
    z1 = jnp.where(row == 0, prev[SUBLANES - 1:SUBLANES], pltpu.roll(z, 1, 0))
    z2 = pltpu.roll(z, 2, 0)
    z2 = jnp.where(row == 0, prev[SUBLANES - 2:SUBLANES - 1], z2)
    z2 = jnp.where(row == 1, prev[SUBLANES - 1:SUBLANES], z2)
    cw = cw_ref[...]
    zc = cw[0:1] * z2 + cw[1:2] * z1 + cw[2:3] * z
    a = (gb_ref[...].astype(F32) * zc).astype(BF16)
    o_ref[...] = x_ref[...] + jnp.dot(a, w_ref[...], preferred_element_type=F32)


def _conv_out(gb, z, conv_w, w_out, layer, x, seq, *, tm=512):
    t, d = x.shape
    cw = jnp.zeros((SUBLANES, d), F32).at[:conv_w.shape[0]].set(conv_w)
    row_spec = pl.BlockSpec((tm, d), lambda i: (i, 0))
    prev_spec = pl.BlockSpec(
        (SUBLANES, d), lambda i: (jnp.maximum(i * (tm // SUBLANES) - 1, 0), 0))
    return pl.pallas_call(
        functools.partial(_conv_out_kernel, tiles_per_seq=seq // tm),
        grid=(t // tm,),
        in_specs=[row_spec, row_spec, prev_spec,
                  pl.BlockSpec((SUBLANES, d), lambda i: (0, 0)),
                  pl.BlockSpec((None, d, d), lambda i: (layer, 0, 0)), row_spec],
        out_specs=row_spec,
        out_shape=jax.ShapeDtypeStruct((t, d), F32),
        compiler_params=_cparams(1), name="conv_out",
    )(gb, z, z, cw, w_out, x)


def _router_kernel(x_ref, g_ref, r_ref, hn_ref, route_ref, gates_ref, counts_ref, carry_ref,
                   *, n_experts):
    i = pl.program_id(0)
    tm = x_ref.shape[0]

    @pl.when(i == 0)
    def _():
        carry_ref[...] = jnp.zeros(carry_ref.shape, F32)

    hn = _rms(x_ref[...], g_ref[...])
    hn_ref[...] = hn
    logits = jnp.dot(hn, r_ref[...], preferred_element_type=F32,
                     precision=lax.Precision.HIGHEST)
    lane = lax.broadcasted_iota(jnp.int32, logits.shape, 1)
    logits = jnp.where(lane < n_experts, logits, -jnp.inf)
    m1 = jnp.max(logits, axis=-1, keepdims=True)
    i1 = jnp.min(jnp.where(logits == m1, lane, LANES), axis=-1, keepdims=True)
    rest = jnp.where(lane == i1, -jnp.inf, logits)
    m2 = jnp.max(rest, axis=-1, keepdims=True)
    i2 = jnp.min(jnp.where(rest == m2, lane, LANES), axis=-1, keepdims=True)
    e2 = jnp.exp(m2 - m1)
    g1 = 1.0 / (1.0 + e2)
    g2 = e2 / (1.0 + e2)

    oh1 = (lane == i1).astype(F32)
    oh2 = (lane == i2).astype(F32)
    r_i = lax.broadcasted_iota(jnp.int32, (tm, tm), 0)
    c_i = lax.broadcasted_iota(jnp.int32, (tm, tm), 1)
    lower = (c_i < r_i).astype(BF16)
    before1 = jnp.dot(lower, oh1.astype(BF16), preferred_element_type=F32)
    before2 = jnp.dot(lower, oh2.astype(BF16), preferred_element_type=F32)
    tot1 = jnp.sum(oh1, axis=0, keepdims=True)
    tot2 = jnp.sum(oh2, axis=0, keepdims=True)
    carry = carry_ref[0:1]
    rank1 = jnp.sum(oh1 * (carry + before1), axis=-1, keepdims=True)
    rank2 = jnp.sum(oh2 * (carry + tot1 + before2), axis=-1, keepdims=True)
    new_carry = carry + tot1 + tot2
    carry_ref[...] = jnp.broadcast_to(new_carry, carry_ref.shape)
    counts_ref[...] = jnp.broadcast_to(new_carry, counts_ref.shape).astype(jnp.int32)

    route = jnp.where(lane == 0, i1, 0) + jnp.where(lane == 1, i2, 0)
    route += jnp.where(lane == 2, rank1.astype(jnp.int32), 0)
    route += jnp.where(lane == 3, rank2.astype(jnp.int32), 0)
    route_ref[...] = route
    gates_ref[...] = jnp.where(lane == 0, g1, 0.0) + jnp.where(lane == 1, g2, 0.0)


def _router(x, gain, router_w, *, tm=512):
    t, d = x.shape
    n_experts = router_w.shape[1]
    r_pad = jnp.zeros((d, LANES), F32).at[:, :n_experts].set(router_w)
    row_spec = pl.BlockSpec((tm, d), lambda i: (i, 0))
    lane_spec = pl.BlockSpec((tm, LANES), lambda i: (i, 0))
    return pl.pallas_call(
        functools.partial(_router_kernel, n_experts=n_experts),
        grid=(t // tm,),
        in_specs=[row_spec, pl.BlockSpec((1, d), lambda i: (0, 0)),
                  pl.BlockSpec((d, LANES), lambda i: (0, 0))],
        out_specs=[row_spec, lane_spec, lane_spec,
                   pl.BlockSpec((SUBLANES, LANES), lambda i: (0, 0))],
        out_shape=[jax.ShapeDtypeStruct((t, d), F32),
                   jax.ShapeDtypeStruct((t, LANES), jnp.int32),
                   jax.ShapeDtypeStruct((t, LANES), F32),
                   jax.ShapeDtypeStruct((SUBLANES, LANES), jnp.int32)],
        scratch_shapes=[pltpu.VMEM((SUBLANES, LANES), F32)],
        compiler_params=_cparams(1), name="moe_router",
    )(x, gain.reshape(1, d), r_pad)


def _dispatch_kernel(pos1_ref, pos2_ref, hn_ref, xs_in_ref, xs_ref, sem):
    del xs_in_ref
    tm = hn_ref.shape[0]
    base = pl.program_id(0) * tm

    def row_copy(r, pos_ref):
        return pltpu.make_async_copy(
            hn_ref.at[pl.ds(r, 1)], xs_ref.at[pl.ds(pos_ref[base + r], 1)], sem)

    def issue(r, carry):
        row_copy(r, pos1_ref).start()
        row_copy(r, pos2_ref).start()
        return carry

    def drain(r, carry):
        row_copy(r, pos1_ref).wait()
        row_copy(r, pos2_ref).wait()
        return carry

    lax.fori_loop(0, tm, issue, 0, unroll=8)
    lax.fori_loop(0, tm, drain, 0, unroll=8)


def _dispatch(hn, pos1, pos2, n_rows, *, tm=256):
    t, d = hn.shape
    xs0 = jnp.zeros((n_rows, d), hn.dtype)
    grid_spec = pltpu.PrefetchScalarGridSpec(
        num_scalar_prefetch=2, grid=(t // tm,),
        in_specs=[pl.BlockSpec((tm, d), lambda i, p1, p2: (i, 0)),
                  pl.BlockSpec(memory_space=pl.ANY)],
        out_specs=pl.BlockSpec(memory_space=pl.ANY),
        scratch_shapes=[pltpu.SemaphoreType.DMA(())])
    return pl.pallas_call(
        _dispatch_kernel, grid_spec=grid_spec,
        out_shape=jax.ShapeDtypeStruct((n_rows, d), hn.dtype),
        input_output_aliases={3: 0},
        compiler_params=_cparams(1), name="moe_dispatch",
    )(pos1, pos2, hn, xs0)


EXPERT_ROW_STEP = 128


def _expert_kernel(tile_ref, te_ref, rows_ref, x_ref, wg_ref, wu_ref, wd_ref, o_ref, xb_ref):
    del tile_ref, te_ref
    i, f = pl.program_id(0), pl.program_id(1)
    tm = x_ref.shape[0]
    rows = rows_ref[i]

    @pl.when(f == 0)
    def _():
        xb_ref[...] = x_ref[...].astype(BF16)
        o_ref[...] = jnp.zeros(o_ref.shape, F32)

    for m in range(EXPERT_ROW_STEP, tm + 1, EXPERT_ROW_STEP):
        @pl.when(rows == m)
        def _():
            o_ref[0:m, :] += _swiglu_partial(xb_ref[0:m, :], wg_ref, wu_ref, wd_ref)


def _experts(xs, tile_idx, tile_expert, tile_rows, wg, wu, wd, layer, *, tm, tf=512):
    n_rows, d = xs.shape
    fe = wg.shape[3]
    nf = fe // tf
    f_eff = lambda i, f, rows: jnp.where(rows[i] > 0, f, nf - 1)
    grid_spec = pltpu.PrefetchScalarGridSpec(
        num_scalar_prefetch=3, grid=(n_rows // tm, nf),
        in_specs=[
            pl.BlockSpec((tm, d), lambda i, f, ti, te, rows: (ti[i], 0)),
            pl.BlockSpec((None, None, d, tf),
                         lambda i, f, ti, te, rows: (layer, te[i], 0, f_eff(i, f, rows))),
            pl.BlockSpec((None, None, d, tf),
                         lambda i, f, ti, te, rows: (layer, te[i], 0, f_eff(i, f, rows))),
            pl.BlockSpec((None, None, tf, d),
                         lambda i, f, ti, te, rows: (layer, te[i], f_eff(i, f, rows), 0)),
        ],
        out_specs=pl.BlockSpec((tm, d), lambda i, f, ti, te, rows: (i, 0)),
        scratch_shapes=[pltpu.VMEM((tm, d), BF16)])
    return pl.pallas_call(
        _expert_kernel, grid_spec=grid_spec,
        out_shape=jax.ShapeDtypeStruct((n_rows, d), F32),
        compiler_params=_cparams(2), name="moe_experts",
    )(tile_idx, tile_expert, tile_rows, xs, wg, wu, wd)


def _combine_kernel(pos1_ref, pos2_ref, x_ref, gates_ref, ys_ref, o_ref, buf_ref, sem):
    tm = x_ref.shape[0]
    base = pl.program_id(0) * tm

    def row_copy(r, k, pos_ref):
        return pltpu.make_async_copy(
            ys_ref.at[pl.ds(pos_ref[base + r], 1)], buf_ref.at[k, pl.ds(r, 1)], sem)

    def issue(r, carry):
        row_copy(r, 0, pos1_ref).start()
        row_copy(r, 1, pos2_ref).start()
        return carry

    def drain(r, carry):
        row_copy(r, 0, pos1_ref).wait()
        row_copy(r, 1, pos2_ref).wait()
        return carry

    lax.fori_loop(0, tm, issue, 0, unroll=8)
    lax.fori_loop(0, tm, drain, 0, unroll=8)
    gates = gates_ref[...]
    o_ref[...] = x_ref[...] + gates[:, 0:1] * buf_ref[0] + gates[:, 1:2] * buf_ref[1]


def _combine(x, gates, ys, pos1, pos2, *, tm=256):
    t, d = x.shape
    grid_spec = pltpu.PrefetchScalarGridSpec(
        num_scalar_prefetch=2, grid=(t // tm,),
        in_specs=[pl.BlockSpec((tm, d), lambda i, p1, p2: (i, 0)),
                  pl.BlockSpec((tm, LANES), lambda i, p1, p2: (i, 0)),
                  pl.BlockSpec(memory_space=pl.ANY)],
        out_specs=pl.BlockSpec((tm, d), lambda i, p1, p2: (i, 0)),
        scratch_shapes=[pltpu.VMEM((TOP_K, tm, d), F32), pltpu.SemaphoreType.DMA(())])
    return pl.pallas_call(
        _combine_kernel, grid_spec=grid_spec,
        out_shape=jax.ShapeDtypeStruct((t, d), F32),
        compiler_params=_cparams(1), name="moe_combine",
    )(pos1, pos2, x, gates, ys)


def _moe_layer(x, gain, router_w, wg, wu, wd, layer, *, tm_e=512):
    t, d = x.shape
    n_experts = router_w.shape[1]
    hn, route, gates, counts = _router(x, gain, router_w)
    counts = counts[0, :n_experts]
    padded = (counts + tm_e - 1) // tm_e * tm_e
    ends = jnp.cumsum(padded)
    starts = ends - padded
    pos1 = starts[route[:, 0]] + route[:, 2]
    pos2 = starts[route[:, 1]] + route[:, 3]
    n_tiles = (t * TOP_K) // tm_e + n_experts
    n_valid = ends[-1] // tm_e
    tiles = jnp.arange(n_tiles, dtype=jnp.int32)
    tile_idx = jnp.minimum(tiles, jnp.maximum(n_valid - 1, 0))
    tile_expert = jnp.sum(tile_idx[:, None] * tm_e >= ends[None, :], axis=1).astype(jnp.int32)
    real_rows = jnp.clip(counts[tile_expert] - (tile_idx * tm_e - starts[tile_expert]), 0, tm_e)
    step = EXPERT_ROW_STEP
    tile_rows = jnp.where(tiles < n_valid, (real_rows + step - 1) // step * step, 0)
    xs = _dispatch(hn, pos1, pos2, n_tiles * tm_e)
    ys = _experts(xs, tile_idx, tile_expert, tile_rows.astype(jnp.int32),
                  wg, wu, wd, layer, tm=tm_e)
    return _combine(x, gates, ys, pos1, pos2)


def kernel(x, attn_norm, attn_w_qkv, attn_q_gain, attn_k_gain, attn_lambda_q1, attn_lambda_k1,
           attn_lambda_q2, attn_lambda_k2, attn_subln_gain, attn_w_o,
           dense_norm, dense_w_gate, dense_w_up, dense_w_down,
           conv_norm, conv_w_in, conv_w, conv_w_out,
           moe_norm, moe_router, moe_w_gate, moe_w_up, moe_w_down):
    batch, seq, d = x.shape
    depth = attn_norm.shape[0] + conv_norm.shape[0]
    tables = _rope_tables(seq)
    h = x.reshape(batch * seq, d)
    bf = lambda w: w.astype(BF16)
    w_qkv, w_o = bf(attn_w_qkv), bf(attn_w_o)
    w_gate, w_up, w_down = bf(dense_w_gate), bf(dense_w_up), bf(dense_w_down)
    w_in, w_out = bf(conv_w_in), bf(conv_w_out)
    e_gate, e_up, e_down = bf(moe_w_gate), bf(moe_w_up), bf(moe_w_down)
    for i in range(depth):
        j = i // 2
        if i % 2 == 0:
            lambda_init = 0.8 - 0.6 * math.exp(-0.3 * i)
            lam_params = jnp.zeros((SUBLANES, HEAD_DIM), F32).at[:4].set(jnp.stack(
                [attn_lambda_q1[j], attn_lambda_k1[j], attn_lambda_q2[j], attn_lambda_k2[j]]))
            qkv = _qkv_proj(h, attn_norm[j], w_qkv, j)
            a = _diff_attention(qkv, lam_params, attn_subln_gain[j], attn_q_gain[j],
                                attn_k_gain[j], tables, batch, seq, lambda_init)
            h = _proj_residual(a, w_o, j, h)
            h = _dense_swiglu(h, dense_norm[j], w_gate, w_up, w_down, j)
        else:
            gb, z = _conv_in(h, conv_norm[j], w_in, j)
            h = _conv_out(gb, z, conv_w[j], w_out, j, h, seq)
            h = _moe_layer(h, moe_norm[j], moe_router[j], e_gate, e_up, e_down, j)
    return h.reshape(batch, seq, d)
```

```python
import functools
import math

import jax
import jax.numpy as jnp
from jax import lax
from jax.experimental import pallas as pl
from jax.experimental.pallas import tpu as pltpu

F32 = jnp.float32
BF16 = jnp.bfloat16

HEAD_DIM = 128
ROT_DIM = HEAD_DIM // 4
ROPE_THETA = 500000.0
RMS_EPS = 1e-6
TOP_K = 2
LANES = 128
SUBLANES = 8
MXU_COLS = 256
VMEM_LIMIT = 56 * 1024 * 1024


def _cparams(n_axes):
    return pltpu.CompilerParams(
        dimension_semantics=("arbitrary",) * n_axes, vmem_limit_bytes=VMEM_LIMIT)


def _rms(x, gain):
    ms = jnp.mean(x * x, axis=-1, keepdims=True)
    return x * lax.rsqrt(ms + RMS_EPS) * gain


def _qkv_kernel(x_ref, g_ref, w_ref, o_ref, hn_ref):
    @pl.when(pl.program_id(1) == 0)
    def _():
        hn_ref[...] = _rms(x_ref[...], g_ref[...]).astype(BF16)

    o_ref[...] = jnp.dot(hn_ref[...], w_ref[...].astype(BF16),
                         preferred_element_type=F32).astype(o_ref.dtype)


def _qkv_proj(x, gain, w, layer, *, tm=1024, tn=512):
    t, d = x.shape
    n = w.shape[2]
    return pl.pallas_call(
        _qkv_kernel,
        grid=(t // tm, n // tn),
        in_specs=[pl.BlockSpec((tm, d), lambda i, j: (i, 0)),
                  pl.BlockSpec((1, d), lambda i, j: (0, 0)),
                  pl.BlockSpec((None, d, tn), lambda i, j: (layer, 0, j))],
        out_specs=pl.BlockSpec((tm, tn), lambda i, j: (i, j)),
        out_shape=jax.ShapeDtypeStruct((t, n), BF16),
        scratch_shapes=[pltpu.VMEM((tm, d), BF16)],
        compiler_params=_cparams(2), name="qkv_proj",
    )(x, gain.reshape(1, d), w)


def _rope_tables(seq):
    inv_freq = ROPE_THETA ** (-jnp.arange(0, ROT_DIM, 2, dtype=F32) / ROT_DIM)
    ang = inv_freq[:, None] * jnp.arange(seq, dtype=F32)[None, :]
    return jnp.cos(ang), jnp.sin(ang)


def _qk_norm_rope_t(x, gain_col, cos, sin):
    half = ROT_DIM // 2
    xt = x.astype(F32).T
    ms = jnp.mean(xt * xt, axis=0, keepdims=True)
    xn = xt * lax.rsqrt(ms + RMS_EPS) * gain_col
    x1, x2 = xn[0:half], xn[half:ROT_DIM]
    rot = jnp.concatenate([x1 * cos - x2 * sin, x2 * cos + x1 * sin, xn[ROT_DIM:]], axis=0)
    return rot.astype(BF16)


def _attn_kernel(q_ref, k_ref, v_ref, lam_ref, sg_ref, qg_ref, kg_ref, cos_ref, sin_ref,
                 o_ref, kt_ref, qt_ref, m_ref, l_ref, acc_ref, sa_ref, sb_ref, mba_ref, mbb_ref,
                 *, lambda_init):
    qi = pl.program_id(2)
    tq = q_ref.shape[0]
    tk = tq
    seq = k_ref.shape[0]
    to_log2 = HEAD_DIM ** -0.5 * math.log2(math.e)

    @pl.when(qi == 0)
    def _():
        def chunk(j, carry):
            r0 = pl.multiple_of(j * tk, tk)
            cos, sin = cos_ref[:, pl.ds(r0, tk)], sin_ref[:, pl.ds(r0, tk)]
            for c in range(2):
                kt_ref[c, :, pl.ds(r0, tk)] = _qk_norm_rope_t(
                    k_ref[pl.ds(r0, tk), c * HEAD_DIM:(c + 1) * HEAD_DIM], kg_ref[...], cos, sin)
            return carry
        lax.fori_loop(0, seq // tk, chunk, 0)

    q0 = pl.multiple_of(qi * tq, tq)
    cos_q, sin_q = cos_ref[:, pl.ds(q0, tq)], sin_ref[:, pl.ds(q0, tq)]
    for c in range(2):
        qt_ref[c] = _qk_norm_rope_t(
            q_ref[:, c * HEAD_DIM:(c + 1) * HEAD_DIM], qg_ref[...], cos_q, sin_q)
    m_ref[...] = jnp.full(m_ref.shape, -jnp.inf, F32)
    l_ref[...] = jnp.zeros(l_ref.shape, F32)
    acc_ref[...] = jnp.zeros(acc_ref.shape, F32)

    def scores(k0, s_ref, mb_ref):
        for c in range(2):
            st = lax.dot_general(kt_ref[c, :, pl.ds(k0, tk)], qt_ref[c],
                                 (((0,), (0,)), ((), ())),
                                 preferred_element_type=F32) * to_log2
            s_ref[c] = st
            mb_ref[c] = jnp.max(st, axis=0, keepdims=True)

    def consume(k0, s_ref, mb_ref, masked):
        v = v_ref[pl.ds(k0, tk), :]
        for c in range(2):
            st = s_ref[c]
            if masked:
                kpos = lax.broadcasted_iota(jnp.int32, (tk, tq), 0)
                qpos = lax.broadcasted_iota(jnp.int32, (tk, tq), 1)
                st = jnp.where(kpos <= qpos, st, -jnp.inf)
                m_blk = jnp.max(st, axis=0, keepdims=True)
            else:
                m_blk = mb_ref[c]
            m_prev = m_ref[c]
            m_new = jnp.maximum(m_prev, m_blk)
            alpha = jnp.exp2(m_prev - m_new)
            pt = jnp.exp2(st - m_new)
            l_ref[c] = alpha * l_ref[c] + jnp.sum(pt, axis=0, keepdims=True)
            pv = lax.dot_general(v, pt.astype(BF16), (((0,), (0,)), ((), ())),
                                 preferred_element_type=F32)
            acc_ref[c] = alpha * acc_ref[c] + pv
            m_ref[c] = m_new

    scores(0, sa_ref, mba_ref)

    def pair(p, carry):
        k0 = pl.multiple_of(2 * p * tk, tk)
        scores(k0 + tk, sb_ref, mbb_ref)
        consume(k0, sa_ref, mba_ref, False)
        scores(k0 + 2 * tk, sa_ref, mba_ref)
        consume(k0 + tk, sb_ref, mbb_ref, False)
        return carry

    lax.fori_loop(0, qi // 2, pair, 0)

    @pl.when(qi % 2 == 0)
    def _():
        consume(q0, sa_ref, mba_ref, True)

    @pl.when(qi % 2 == 1)
    def _():
        scores(q0, sb_ref, mbb_ref)
        consume(q0 - tk, sa_ref, mba_ref, False)
        consume(q0, sb_ref, mbb_ref, True)

    lp = lam_ref[...]
    lam = (jnp.exp(jnp.sum(lp[0:1] * lp[1:2], axis=-1, keepdims=True))
           - jnp.exp(jnp.sum(lp[2:3] * lp[3:4], axis=-1, keepdims=True))
           + lambda_init)
    ot = acc_ref[0] * (1.0 / l_ref[0]) - lam * (acc_ref[1] * (1.0 / l_ref[1]))
    ms = jnp.mean(ot * ot, axis=0, keepdims=True)
    on = ot * lax.rsqrt(ms + RMS_EPS) * sg_ref[...] * (1.0 - lambda_init)
    o_ref[...] = on.T.astype(o_ref.dtype)


def _diff_attention(qkv, lam_params, subln_gain, q_gain, k_gain, tables, batch, seq,
                    lambda_init, *, tq=512):
    t, n3 = qkv.shape
    d = n3 // 3
    vd = 2 * HEAD_DIM
    heads = d // vd
    nq = seq // tq
    const = lambda shape: pl.BlockSpec(shape, lambda b, h, qi: (0, 0))
    return pl.pallas_call(
        functools.partial(_attn_kernel, lambda_init=lambda_init),
        grid=(batch, heads, nq),
        in_specs=[
            pl.BlockSpec((tq, vd), lambda b, h, qi: (b * nq + qi, h)),
            pl.BlockSpec((seq, vd), lambda b, h, qi: (b, heads + h)),
            pl.BlockSpec((seq, vd), lambda b, h, qi: (b, 2 * heads + h)),
            const((SUBLANES, HEAD_DIM)), const((vd, 1)),
            const((HEAD_DIM, 1)), const((HEAD_DIM, 1)),
            const((ROT_DIM // 2, seq)), const((ROT_DIM // 2, seq)),
        ],
        out_specs=pl.BlockSpec((tq, vd), lambda b, h, qi: (b * nq + qi, h)),
        out_shape=jax.ShapeDtypeStruct((t, d), BF16),
        scratch_shapes=[pltpu.VMEM((2, HEAD_DIM, seq), BF16),
                        pltpu.VMEM((2, HEAD_DIM, tq), BF16),
                        pltpu.VMEM((2, 1, tq), F32), pltpu.VMEM((2, 1, tq), F32),
                        pltpu.VMEM((2, vd, tq), F32),
                        pltpu.VMEM((2, tq, tq), F32), pltpu.VMEM((2, tq, tq), F32),
                        pltpu.VMEM((2, 1, tq), F32), pltpu.VMEM((2, 1, tq), F32)],
        compiler_params=_cparams(3), name="diff_attention",
    )(qkv, qkv, qkv, lam_params, subln_gain.reshape(vd, 1),
      q_gain.reshape(HEAD_DIM, 1), k_gain.reshape(HEAD_DIM, 1), *tables)


def _proj_residual_kernel(a_ref, w_ref, x_ref, o_ref):
    o_ref[...] = x_ref[...] + jnp.dot(a_ref[...], w_ref[...], preferred_element_type=F32)


def _proj_residual(a, w, layer, x, *, tm=512):
    t, d = x.shape
    k = a.shape[1]
    return pl.pallas_call(
        _proj_residual_kernel,
        grid=(t // tm,),
        in_specs=[pl.BlockSpec((tm, k), lambda i: (i, 0)),
                  pl.BlockSpec((None, k, d), lambda i: (layer, 0, 0)),
                  pl.BlockSpec((tm, d), lambda i: (i, 0))],
        out_specs=pl.BlockSpec((tm, d), lambda i: (i, 0)),
        out_shape=jax.ShapeDtypeStruct((t, d), F32),
        compiler_params=_cparams(1), name="proj_residual",
    )(a, w, x)


def _swiglu_partial(h, wg_ref, wu_ref, wd_ref):
    gate = jnp.dot(h, wg_ref[...].astype(BF16), preferred_element_type=F32)
    up = jnp.dot(h, wu_ref[...].astype(BF16), preferred_element_type=F32)
    act = (gate * jax.nn.sigmoid(gate) * up).astype(BF16)
    return jnp.dot(act, wd_ref[...].astype(BF16), preferred_element_type=F32)


def _dense_kernel(x_ref, g_ref, wg_ref, wu_ref, wd_ref, o_ref, hn_ref):
    f = pl.program_id(1)

    @pl.when(f == 0)
    def _():
        x = x_ref[...]
        hn_ref[...] = _rms(x, g_ref[...]).astype(BF16)
        o_ref[...] = x

    o_ref[...] += _swiglu_partial(hn_ref[...], wg_ref, wu_ref, wd_ref)


def _dense_swiglu(x, gain, wg, wu, wd, layer, *, tm=1024, tf=256):
    t, d = x.shape
    ff = wg.shape[2]
    return pl.pallas_call(
        _dense_kernel,
        grid=(t // tm, ff // tf),
        in_specs=[pl.BlockSpec((tm, d), lambda i, f: (i, 0), pipeline_mode=pl.Buffered(1)),
                  pl.BlockSpec((1, d), lambda i, f: (0, 0)),
                  pl.BlockSpec((None, d, tf), lambda i, f: (layer, 0, f)),
                  pl.BlockSpec((None, d, tf), lambda i, f: (layer, 0, f)),
                  pl.BlockSpec((None, tf, d), lambda i, f: (layer, f, 0))],
        out_specs=pl.BlockSpec((tm, d), lambda i, f: (i, 0)),
        out_shape=jax.ShapeDtypeStruct((t, d), F32),
        scratch_shapes=[pltpu.VMEM((tm, d), BF16)],
        compiler_params=_cparams(2), name="dense_swiglu",
    )(x, gain.reshape(1, d), wg, wu, wd)


def _conv_in_kernel(x_ref, g_ref, wb_ref, wc_ref, wu_ref, gb_ref, z_ref, hn_ref):
    j = pl.program_id(1)

    @pl.when(j == 0)
    def _():
        hn_ref[...] = _rms(x_ref[...], g_ref[...]).astype(BF16)

    h = hn_ref[...]
    proj = lambda w_ref: jnp.dot(h, w_ref[...].astype(BF16), preferred_element_type=F32)
    gb_ref[...] = proj(wb_ref).astype(gb_ref.dtype)
    gc = proj(wc_ref)
    u = proj(wu_ref)
    z_ref[...] = (gc * u).astype(z_ref.dtype)


def _conv_in(x, gain, w_in, layer, *, tm=1024, tn=512):
    t, d = x.shape
    nj = d // tn
    w_spec = lambda part: pl.BlockSpec((None, d, tn), lambda i, j: (layer, 0, part * nj + j))
    out_spec = pl.BlockSpec((tm, tn), lambda i, j: (i, j))
    return pl.pallas_call(
        _conv_in_kernel,
        grid=(t // tm, nj),
        in_specs=[pl.BlockSpec((tm, d), lambda i, j: (i, 0)),
                  pl.BlockSpec((1, d), lambda i, j: (0, 0)),
                  w_spec(0), w_spec(1), w_spec(2)],
        out_specs=[out_spec, out_spec],
        out_shape=[jax.ShapeDtypeStruct((t, d), BF16), jax.ShapeDtypeStruct((t, d), BF16)],
        scratch_shapes=[pltpu.VMEM((tm, d), BF16)],
        compiler_params=_cparams(2), name="conv_in",
    )(x, gain.reshape(1, d), w_in, w_in, w_in)


def _conv_out_kernel(gb_ref, z_ref, zprev_ref, cw_ref, w_ref, x_ref, o_ref, *, tiles_per_seq):
    i = pl.program_id(0)
    z = z_ref[...].astype(F32)
    prev = jnp.where(i % tiles_per_seq == 0, 0.0, zprev_ref[...].astype(F32))
    row = lax.broadcasted_iota(jnp.int32, z.shape, 0)
    z1 = jnp.where(row == 0, prev[SUBLANES - 1:SUBLANES], pltpu.roll(z, 1, 0))
    z2 = pltpu.roll(z, 2, 0)
    z2 = jnp.where(row == 0, prev[SUBLANES - 2:SUBLANES - 1], z2)
    z2 = jnp.where(row == 1, prev[SUBLANES - 1:SUBLANES], z2)
    cw = cw_ref[...]
    zc = cw[0:1] * z2 + cw[1:2] * z1 + cw[2:3] * z
    a = (gb_ref[...].astype(F32) * zc).astype(BF16)
    o_ref[...] = x_ref[...] + jnp.dot(a, w_ref[...], preferred_element_type=F32)


def _conv_out(gb, z, conv_w, w_out, layer, x, seq, *, tm=512):
    t, d = x.shape
    cw = jnp.zeros((SUBLANES, d), F32).at[:conv_w.shape[0]].set(conv_w)
    row_spec = pl.BlockSpec((tm, d), lambda i: (i, 0))
    prev_spec = pl.BlockSpec(
        (SUBLANES, d), lambda i: (jnp.maximum(i * (tm // SUBLANES) - 1, 0), 0))
    return pl.pallas_call(
        functools.partial(_conv_out_kernel, tiles_per_seq=seq // tm),
        grid=(t // tm,),
        in_specs=[row_spec, row_spec, prev_spec,
                  pl.BlockSpec((SUBLANES, d), lambda i: (0, 0)),
                  pl.BlockSpec((None, d, d), lambda i: (layer, 0, 0)), row_spec],
        out_specs=row_spec,
        out_shape=jax.ShapeDtypeStruct((t, d), F32),
        compiler_params=_cparams(1), name="conv_out",
    )(gb, z, z, cw, w_out, x)


def _router_kernel(x_ref, g_ref, r_ref, hn_ref, route_ref, gates_ref, counts_ref, carry_ref,
                   *, n_experts):
    i = pl.program_id(0)
    tm = x_ref.shape[0]

    @pl.when(i == 0)
    def _():
        carry_ref[...] = jnp.zeros(carry_ref.shape, F32)

    hn = _rms(x_ref[...], g_ref[...])
    hn_ref[...] = hn
    logits = jnp.dot(hn, r_ref[...], preferred_element_type=F32,
                     precision=lax.Precision.HIGHEST)
    lane = lax.broadcasted_iota(jnp.int32, logits.shape, 1)
    logits = jnp.where(lane < n_experts, logits, -jnp.inf)
    m1 = jnp.max(logits, axis=-1, keepdims=True)
    i1 = jnp.min(jnp.where(logits == m1, lane, LANES), axis=-1, keepdims=True)
    rest = jnp.where(lane == i1, -jnp.inf, logits)
    m2 = jnp.max(rest, axis=-1, keepdims=True)
    i2 = jnp.min(jnp.where(rest == m2, lane, LANES), axis=-1, keepdims=True)
    e2 = jnp.exp(m2 - m1)
    g1 = 1.0 / (1.0 + e2)
    g2 = e2 / (1.0 + e2)

    oh1 = (lane == i1).astype(F32)
    oh2 = (lane == i2).astype(F32)
    r_i = lax.broadcasted_iota(jnp.int32, (tm, tm), 0)
    c_i = lax.broadcasted_iota(jnp.int32, (tm, tm), 1)
    lower = (c_i < r_i).astype(BF16)
    before1 = jnp.dot(lower, oh1.astype(BF16), preferred_element_type=F32)
    before2 = jnp.dot(lower, oh2.astype(BF16), preferred_element_type=F32)
    tot1 = jnp.sum(oh1, axis=0, keepdims=True)
    tot2 = jnp.sum(oh2, axis=0, keepdims=True)
    carry = carry_ref[0:1]
    rank1 = jnp.sum(oh1 * (carry + before1), axis=-1, keepdims=True)
    rank2 = jnp.sum(oh2 * (carry + tot1 + before2), axis=-1, keepdims=True)
    new_carry = carry + tot1 + tot2
    carry_ref[...] = jnp.broadcast_to(new_carry, carry_ref.shape)
    counts_ref[...] = jnp.broadcast_to(new_carry, counts_ref.shape).astype(jnp.int32)

    route = jnp.where(lane == 0, i1, 0) + jnp.where(lane == 1, i2, 0)
    route += jnp.where(lane == 2, rank1.astype(jnp.int32), 0)
    route += jnp.where(lane == 3, rank2.astype(jnp.int32), 0)
    route_ref[...] = route
    gates_ref[...] = jnp.where(lane == 0, g1, 0.0) + jnp.where(lane == 1, g2, 0.0)


def _router(x, gain, router_w, *, tm=512):
    t, d = x.shape
    n_experts = router_w.shape[1]
    r_pad = jnp.zeros((d, LANES), F32).at[:, :n_experts].set(router_w)
    row_spec = pl.BlockSpec((tm, d), lambda i: (i, 0))
    lane_spec = pl.BlockSpec((tm, LANES), lambda i: (i, 0))
    return pl.pallas_call(
        functools.partial(_router_kernel, n_experts=n_experts),
        grid=(t // tm,),
        in_specs=[row_spec, pl.BlockSpec((1, d), lambda i: (0, 0)),
                  pl.BlockSpec((d, LANES), lambda i: (0, 0))],
        out_specs=[row_spec, lane_spec, lane_spec,
                   pl.BlockSpec((SUBLANES, LANES), lambda i: (0, 0))],
        out_shape=[jax.ShapeDtypeStruct((t, d), F32),
                   jax.ShapeDtypeStruct((t, LANES), jnp.int32),
                   jax.ShapeDtypeStruct((t, LANES), F32),
                   jax.ShapeDtypeStruct((SUBLANES, LANES), jnp.int32)],
        scratch_shapes=[pltpu.VMEM((SUBLANES, LANES), F32)],
        compiler_params=_cparams(1), name="moe_router",
    )(x, gain.reshape(1, d), r_pad)


def _dispatch_kernel(pos1_ref, pos2_ref, hn_ref, xs_in_ref, xs_ref, sem):
    del xs_in_ref
    tm = hn_ref.shape[0]
    base = pl.program_id(0) * tm

    def row_copy(r, pos_ref):
        return pltpu.make_async_copy(
            hn_ref.at[pl.ds(r, 1)], xs_ref.at[pl.ds(pos_ref[base + r], 1)], sem)

    def issue(r, carry):
        row_copy(r, pos1_ref).start()
        row_copy(r, pos2_ref).start()
        return carry

    def drain(r, carry):
        row_copy(r, pos1_ref).wait()
        row_copy(r, pos2_ref).wait()
        return carry

    lax.fori_loop(0, tm, issue, 0, unroll=8)
    lax.fori_loop(0, tm, drain, 0, unroll=8)


def _dispatch(hn, pos1, pos2, n_rows, *, tm=256):
    t, d = hn.shape
    xs0 = jnp.zeros((n_rows, d), hn.dtype)
    grid_spec = pltpu.PrefetchScalarGridSpec(
        num_scalar_prefetch=2, grid=(t // tm,),
        in_specs=[pl.BlockSpec((tm, d), lambda i, p1, p2: (i, 0)),
                  pl.BlockSpec(memory_space=pl.ANY)],
        out_specs=pl.BlockSpec(memory_space=pl.ANY),
        scratch_shapes=[pltpu.SemaphoreType.DMA(())])
    return pl.pallas_call(
        _dispatch_kernel, grid_spec=grid_spec,
        out_shape=jax.ShapeDtypeStruct((n_rows, d), hn.dtype),
        input_output_aliases={3: 0},
        compiler_params=_cparams(1), name="moe_dispatch",
    )(pos1, pos2, hn, xs0)


EXPERT_ROW_STEP = 128


def _expert_kernel(tile_ref, te_ref, rows_ref, x_ref, wg_ref, wu_ref, wd_ref, o_ref, xb_ref):
    del tile_ref, te_ref
    i, f = pl.program_id(0), pl.program_id(1)
    tm = x_ref.shape[0]
    rows = rows_ref[i]

    @pl.when(f == 0)
    def _():
        xb_ref[...] = x_ref[...].astype(BF16)
        o_ref[...] = jnp.zeros(o_ref.shape, F32)

    for m in range(EXPERT_ROW_STEP, tm + 1, EXPERT_ROW_STEP):
        @pl.when(rows == m)
        def _():
            o_ref[0:m, :] += _swiglu_partial(xb_ref[0:m, :], wg_ref, wu_ref, wd_ref)


def _experts(xs, tile_idx, tile_expert, tile_rows, wg, wu, wd, layer, *, tm, tf=256):
    n_rows, d = xs.shape
    fe = wg.shape[3]
    nf = fe // tf
    f_eff = lambda i, f, rows: jnp.where(rows[i] > 0, f, nf - 1)
    grid_spec = pltpu.PrefetchScalarGridSpec(
        num_scalar_prefetch=3, grid=(n_rows // tm, nf),
        in_specs=[
            pl.BlockSpec((tm, d), lambda i, f, ti, te, rows: (ti[i], 0),
                         pipeline_mode=pl.Buffered(1)),
            pl.BlockSpec((None, None, d, tf),
                         lambda i, f, ti, te, rows: (layer, te[i], 0, f_eff(i, f, rows))),
            pl.BlockSpec((None, None, d, tf),
                         lambda i, f, ti, te, rows: (layer, te[i], 0, f_eff(i, f, rows))),
            pl.BlockSpec((None, None, tf, d),
                         lambda i, f, ti, te, rows: (layer, te[i], f_eff(i, f, rows), 0)),
        ],
        out_specs=pl.BlockSpec((tm, d), lambda i, f, ti, te, rows: (i, 0)),
        scratch_shapes=[pltpu.VMEM((tm, d), BF16)])
    return pl.pallas_call(
        _expert_kernel, grid_spec=grid_spec,
        out_shape=jax.ShapeDtypeStruct((n_rows, d), F32),
        compiler_params=_cparams(2), name="moe_experts",
    )(tile_idx, tile_expert, tile_rows, xs, wg, wu, wd)


def _combine_kernel(pos1_ref, pos2_ref, x_ref, gates_ref, ys_ref, o_ref, buf_ref, sem):
    tm = x_ref.shape[0]
    base = pl.program_id(0) * tm

    def row_copy(r, k, pos_ref):
        return pltpu.make_async_copy(
            ys_ref.at[pl.ds(pos_ref[base + r], 1)], buf_ref.at[k, pl.ds(r, 1)], sem)

    def issue(r, carry):
        row_copy(r, 0, pos1_ref).start()
        row_copy(r, 1, pos2_ref).start()
        return carry

    def drain(r, carry):
        row_copy(r, 0, pos1_ref).wait()
        row_copy(r, 1, pos2_ref).wait()
        return carry

    lax.fori_loop(0, tm, issue, 0, unroll=8)
    lax.fori_loop(0, tm, drain, 0, unroll=8)
    gates = gates_ref[...]
    o_ref[...] = x_ref[...] + gates[:, 0:1] * buf_ref[0] + gates[:, 1:2] * buf_ref[1]


def _combine(x, gates, ys, pos1, pos2, *, tm=256):
    t, d = x.shape
    grid_spec = pltpu.PrefetchScalarGridSpec(
        num_scalar_prefetch=2, grid=(t // tm,),
        in_specs=[pl.BlockSpec((tm, d), lambda i, p1, p2: (i, 0)),
                  pl.BlockSpec((tm, LANES), lambda i, p1, p2: (i, 0)),
                  pl.BlockSpec(memory_space=pl.ANY)],
        out_specs=pl.BlockSpec((tm, d), lambda i, p1, p2: (i, 0)),
        scratch_shapes=[pltpu.VMEM((TOP_K, tm, d), F32), pltpu.SemaphoreType.DMA(())])
    return pl.pallas_call(
        _combine_kernel, grid_spec=grid_spec,
        out_shape=jax.ShapeDtypeStruct((t, d), F32),
        compiler_params=_cparams(1), name="moe_combine",
    )(pos1, pos2, x, gates, ys)


def _moe_layer(x, gain, router_w, wg, wu, wd, layer, *, tm_e=1152):
    t, d = x.shape
    n_experts = router_w.shape[1]
    hn, route, gates, counts = _router(x, gain, router_w)
    counts = counts[0, :n_experts]
    padded = (counts + tm_e - 1) // tm_e * tm_e
    ends = jnp.cumsum(padded)
    starts = ends - padded
    pos1 = starts[route[:, 0]] + route[:, 2]
    pos2 = starts[route[:, 1]] + route[:, 3]
    n_tiles = (t * TOP_K) // tm_e + n_experts
    n_valid = ends[-1] // tm_e
    tiles = jnp.arange(n_tiles, dtype=jnp.int32)
    tile_idx = jnp.minimum(tiles, jnp.maximum(n_valid - 1, 0))
    tile_expert = jnp.sum(tile_idx[:, None] * tm_e >= ends[None, :], axis=1).astype(jnp.int32)
    real_rows = jnp.clip(counts[tile_expert] - (tile_idx * tm_e - starts[tile_expert]), 0, tm_e)
    step = EXPERT_ROW_STEP
    tile_rows = jnp.where(tiles < n_valid, (real_rows + step - 1) // step * step, 0)
    xs = _dispatch(hn, pos1, pos2, n_tiles * tm_e)
    ys = _experts(xs, tile_idx, tile_expert, tile_rows.astype(jnp.int32),
                  wg, wu, wd, layer, tm=tm_e)
    return _combine(x, gates, ys, pos1, pos2)


def kernel(x, attn_norm, attn_w_qkv, attn_q_gain, attn_k_gain, attn_lambda_q1, attn_lambda_k1,
           attn_lambda_q2, attn_lambda_k2, attn_subln_gain, attn_w_o,
           dense_norm, dense_w_gate, dense_w_up, dense_w_down,
           conv_norm, conv_w_in, conv_w, conv_w_out,
           moe_norm, moe_router, moe_w_gate, moe_w_up, moe_w_down):
    batch, seq, d = x.shape
    depth = attn_norm.shape[0] + conv_norm.shape[0]
    tables = _rope_tables(seq)
    h = x.reshape(batch * seq, d)
    w_o, w_out = attn_w_o.astype(BF16), conv_w_out.astype(BF16)
    for i in range(depth):
        j = i // 2
        if i % 2 == 0:
            lambda_init = 0.8 - 0.6 * math.exp(-0.3 * i)
            lam_params = jnp.zeros((SUBLANES, HEAD_DIM), F32).at[:4].set(jnp.stack(
                [attn_lambda_q1[j], attn_lambda_k1[j], attn_lambda_q2[j], attn_lambda_k2[j]]))
            qkv = _qkv_proj(h, attn_norm[j], attn_w_qkv, j)
            a = _diff_attention(qkv, lam_params, attn_subln_gain[j], attn_q_gain[j],
                                attn_k_gain[j], tables, batch, seq, lambda_init)
            h = _proj_residual(a, w_o, j, h)
            h = _dense_swiglu(h, dense_norm[j], dense_w_gate, dense_w_up, dense_w_down, j)
        else:
            gb, z = _conv_in(h, conv_norm[j], conv_w_in, j)
            h = _conv_out(gb, z, conv_w[j], w_out, j, h, seq)
            h = _moe_layer(h, moe_norm[j], moe_router[j], moe_w_gate, moe_w_up, moe_w_down, j)
    return h.reshape(batch, seq, d)
```

```python
import functools
import math

import jax
import jax.numpy as jnp
from jax import lax
from jax.experimental import pallas as pl
from jax.experimental.pallas import tpu as pltpu

F32 = jnp.float32
BF16 = jnp.bfloat16

HEAD_DIM = 128
ROT_DIM = HEAD_DIM // 4
ROPE_THETA = 500000.0
RMS_EPS = 1e-6
TOP_K = 2
LANES = 128
SUBLANES = 8
MXU_COLS = 256
VMEM_LIMIT = 56 * 1024 * 1024


def _cparams(n_axes):
    return pltpu.CompilerParams(
        dimension_semantics=("arbitrary",) * n_axes, vmem_limit_bytes=VMEM_LIMIT)


def _rms(x, gain):
    ms = jnp.mean(x * x, axis=-1, keepdims=True)
    return x * lax.rsqrt(ms + RMS_EPS) * gain


def _qkv_kernel(x_ref, g_ref, w_ref, o_ref, hn_ref):
    @pl.when(pl.program_id(1) == 0)
    def _():
        hn_ref[...] = _rms(x_ref[...], g_ref[...]).astype(BF16)

    o_ref[...] = jnp.dot(hn_ref[...], w_ref[...].astype(BF16),
                         preferred_element_type=F32).astype(o_ref.dtype)


def _qkv_proj(x, gain, w, layer, *, tm=1024, tn=512):
    t, d = x.shape
    n = w.shape[2]
    return pl.pallas_call(
        _qkv_kernel,
        grid=(t // tm, n // tn),
        in_specs=[pl.BlockSpec((tm, d), lambda i, j: (i, 0)),
                  pl.BlockSpec((1, d), lambda i, j: (0, 0)),
                  pl.BlockSpec((None, d, tn), lambda i, j: (layer, 0, j))],
        out_specs=pl.BlockSpec((tm, tn), lambda i, j: (i, j)),
        out_shape=jax.ShapeDtypeStruct((t, n), BF16),
        scratch_shapes=[pltpu.VMEM((tm, d), BF16)],
        compiler_params=_cparams(2), name="qkv_proj",
    )(x, gain.reshape(1, d), w)


def _rope_tables(seq):
    inv_freq = ROPE_THETA ** (-jnp.arange(0, ROT_DIM, 2, dtype=F32) / ROT_DIM)
    ang = inv_freq[:, None] * jnp.arange(seq, dtype=F32)[None, :]
    return jnp.cos(ang), jnp.sin(ang)


def _qk_norm_rope_t(x, gain_col, cos, sin):
    half = ROT_DIM // 2
    xt = x.astype(F32).T
    ms = jnp.mean(xt * xt, axis=0, keepdims=True)
    xn = xt * lax.rsqrt(ms + RMS_EPS) * gain_col
    x1, x2 = xn[0:half], xn[half:ROT_DIM]
    rot = jnp.concatenate([x1 * cos - x2 * sin, x2 * cos + x1 * sin, xn[ROT_DIM:]], axis=0)
    return rot.astype(BF16)


def _attn_kernel(q_ref, k_ref, v_ref, lam_ref, sg_ref, qg_ref, kg_ref, cos_ref, sin_ref,
                 o_ref, kt_ref, qt_ref, m_ref, l_ref, acc_ref, sa_ref, sb_ref, mba_ref, mbb_ref,
                 *, lambda_init):
    qi = pl.program_id(2)
    tq = q_ref.shape[0]
    tk = tq
    seq = k_ref.shape[0]
    to_log2 = HEAD_DIM ** -0.5 * math.log2(math.e)

    @pl.when(qi == 0)
    def _():
        def chunk(j, carry):
            r0 = pl.multiple_of(j * tk, tk)
            cos, sin = cos_ref[:, pl.ds(r0, tk)], sin_ref[:, pl.ds(r0, tk)]
            for c in range(2):
                kt_ref[c, :, pl.ds(r0, tk)] = _qk_norm_rope_t(
                    k_ref[pl.ds(r0, tk), c * HEAD_DIM:(c + 1) * HEAD_DIM], kg_ref[...], cos, sin)
            return carry
        lax.fori_loop(0, seq // tk, chunk, 0)

    q0 = pl.multiple_of(qi * tq, tq)
    cos_q, sin_q = cos_ref[:, pl.ds(q0, tq)], sin_ref[:, pl.ds(q0, tq)]
    for c in range(2):
        qt_ref[c] = _qk_norm_rope_t(
            q_ref[:, c * HEAD_DIM:(c + 1) * HEAD_DIM], qg_ref[...], cos_q, sin_q)
    m_ref[...] = jnp.full(m_ref.shape, -jnp.inf, F32)
    l_ref[...] = jnp.zeros(l_ref.shape, F32)
    acc_ref[...] = jnp.zeros(acc_ref.shape, F32)

    def scores(k0, s_ref, mb_ref):
        for c in range(2):
            st = lax.dot_general(kt_ref[c, :, pl.ds(k0, tk)], qt_ref[c],
                                 (((0,), (0,)), ((), ())),
                                 preferred_element_type=F32) * to_log2
            s_ref[c] = st
            mb_ref[c] = jnp.max(st, axis=0, keepdims=True)

    def consume(k0, s_ref, mb_ref, masked):
        v = v_ref[pl.ds(k0, tk), :]
        for c in range(2):
            st = s_ref[c]
            if masked:
                kpos = lax.broadcasted_iota(jnp.int32, (tk, tq), 0)
                qpos = lax.broadcasted_iota(jnp.int32, (tk, tq), 1)
                st = jnp.where(kpos <= qpos, st, -jnp.inf)
                m_blk = jnp.max(st, axis=0, keepdims=True)
            else:
                m_blk = mb_ref[c]
            m_prev = m_ref[c]
            m_new = jnp.maximum(m_prev, m_blk)
            alpha = jnp.exp2(m_prev - m_new)
            pt = jnp.exp2(st - m_new)
            l_ref[c] = alpha * l_ref[c] + jnp.sum(pt, axis=0, keepdims=True)
            pv = lax.dot_general(v, pt.astype(BF16), (((0,), (0,)), ((), ())),
                                 preferred_element_type=F32)
            acc_ref[c] = alpha * acc_ref[c] + pv
            m_ref[c] = m_new

    scores(0, sa_ref, mba_ref)

    def pair(p, carry):
        k0 = pl.multiple_of(2 * p * tk, tk)
        scores(k0 + tk, sb_ref, mbb_ref)
        consume(k0, sa_ref, mba_ref, False)
        scores(k0 + 2 * tk, sa_ref, mba_ref)
        consume(k0 + tk, sb_ref, mbb_ref, False)
        return carry

    lax.fori_loop(0, qi // 2, pair, 0)

    @pl.when(qi % 2 == 0)
    def _():
        consume(q0, sa_ref, mba_ref, True)

    @pl.when(qi % 2 == 1)
    def _():
        scores(q0, sb_ref, mbb_ref)
        consume(q0 - tk, sa_ref, mba_ref, False)
        consume(q0, sb_ref, mbb_ref, True)

    lp = lam_ref[...]
    lam = (jnp.exp(jnp.sum(lp[0:1] * lp[1:2], axis=-1, keepdims=True))
           - jnp.exp(jnp.sum(lp[2:3] * lp[3:4], axis=-1, keepdims=True))
           + lambda_init)
    ot = acc_ref[0] * (1.0 / l_ref[0]) - lam * (acc_ref[1] * (1.0 / l_ref[1]))
    ms = jnp.mean(ot * ot, axis=0, keepdims=True)
    on = ot * lax.rsqrt(ms + RMS_EPS) * sg_ref[...] * (1.0 - lambda_init)
    o_ref[...] = on.T.astype(o_ref.dtype)


def _diff_attention(qkv, lam_params, subln_gain, q_gain, k_gain, tables, batch, seq,
                    lambda_init, *, tq=512):
    t, n3 = qkv.shape
    d = n3 // 3
    vd = 2 * HEAD_DIM
    heads = d // vd
    nq = seq // tq
    const = lambda shape: pl.BlockSpec(shape, lambda b, h, qi: (0, 0))
    return pl.pallas_call(
        functools.partial(_attn_kernel, lambda_init=lambda_init),
        grid=(batch, heads, nq),
        in_specs=[
            pl.BlockSpec((tq, vd), lambda b, h, qi: (b * nq + qi, h)),
            pl.BlockSpec((seq, vd), lambda b, h, qi: (b, heads + h)),
            pl.BlockSpec((seq, vd), lambda b, h, qi: (b, 2 * heads + h)),
            const((SUBLANES, HEAD_DIM)), const((vd, 1)),
            const((HEAD_DIM, 1)), const((HEAD_DIM, 1)),
            const((ROT_DIM // 2, seq)), const((ROT_DIM // 2, seq)),
        ],
        out_specs=pl.BlockSpec((tq, vd), lambda b, h, qi: (b * nq + qi, h)),
        out_shape=jax.ShapeDtypeStruct((t, d), BF16),
        scratch_shapes=[pltpu.VMEM((2, HEAD_DIM, seq), BF16),
                        pltpu.VMEM((2, HEAD_DIM, tq), BF16),
                        pltpu.VMEM((2, 1, tq), F32), pltpu.VMEM((2, 1, tq), F32),
                        pltpu.VMEM((2, vd, tq), F32),
                        pltpu.VMEM((2, tq, tq), F32), pltpu.VMEM((2, tq, tq), F32),
                        pltpu.VMEM((2, 1, tq), F32), pltpu.VMEM((2, 1, tq), F32)],
        compiler_params=_cparams(3), name="diff_attention",
    )(qkv, qkv, qkv, lam_params, subln_gain.reshape(vd, 1),
      q_gain.reshape(HEAD_DIM, 1), k_gain.reshape(HEAD_DIM, 1), *tables)


def _proj_residual_kernel(a_ref, w_ref, x_ref, o_ref):
    o_ref[...] = x_ref[...] + jnp.dot(a_ref[...], w_ref[...], preferred_element_type=F32)


def _proj_residual(a, w, layer, x, *, tm=512):
    t, d = x.shape
    k = a.shape[1]
    return pl.pallas_call(
        _proj_residual_kernel,
        grid=(t // tm,),
        in_specs=[pl.BlockSpec((tm, k), lambda i: (i, 0)),
                  pl.BlockSpec((None, k, d), lambda i: (layer, 0, 0)),
                  pl.BlockSpec((tm, d), lambda i: (i, 0))],
        out_specs=pl.BlockSpec((tm, d), lambda i: (i, 0)),
        out_shape=jax.ShapeDtypeStruct((t, d), F32),
        compiler_params=_cparams(1), name="proj_residual",
    )(a, w, x)


def _swiglu_partial(h, wg_ref, wu_ref, wd_ref):
    gate = jnp.dot(h, wg_ref[...].astype(BF16), preferred_element_type=F32)
    up = jnp.dot(h, wu_ref[...].astype(BF16), preferred_element_type=F32)
    act = (gate * jax.nn.sigmoid(gate) * up).astype(BF16)
    return jnp.dot(act, wd_ref[...].astype(BF16), preferred_element_type=F32)


def _dense_kernel(x_ref, g_ref, wg_ref, wu_ref, wd_ref, o_ref, hn_ref):
    f = pl.program_id(1)

    @pl.when(f == 0)
    def _():
        x = x_ref[...]
        hn_ref[...] = _rms(x, g_ref[...]).astype(BF16)
        o_ref[...] = x

    o_ref[...] += _swiglu_partial(hn_ref[...], wg_ref, wu_ref, wd_ref)


def _dense_swiglu(x, gain, wg, wu, wd, layer, *, tm=1024, tf=256):
    t, d = x.shape
    ff = wg.shape[2]
    return pl.pallas_call(
        _dense_kernel,
        grid=(t // tm, ff // tf),
        in_specs=[pl.BlockSpec((tm, d), lambda i, f: (i, 0), pipeline_mode=pl.Buffered(1)),
                  pl.BlockSpec((1, d), lambda i, f: (0, 0)),
                  pl.BlockSpec((None, d, tf), lambda i, f: (layer, 0, f)),
                  pl.BlockSpec((None, d, tf), lambda i, f: (layer, 0, f)),
                  pl.BlockSpec((None, tf, d), lambda i, f: (layer, f, 0))],
        out_specs=pl.BlockSpec((tm, d), lambda i, f: (i, 0)),
        out_shape=jax.ShapeDtypeStruct((t, d), F32),
        scratch_shapes=[pltpu.VMEM((tm, d), BF16)],
        compiler_params=_cparams(2), name="dense_swiglu",
    )(x, gain.reshape(1, d), wg, wu, wd)


def _conv_in_kernel(x_ref, g_ref, wb_ref, wc_ref, wu_ref, gb_ref, z_ref, hn_ref):
    j = pl.program_id(1)

    @pl.when(j == 0)
    def _():
        hn_ref[...] = _rms(x_ref[...], g_ref[...]).astype(BF16)

    h = hn_ref[...]
    proj = lambda w_ref: jnp.dot(h, w_ref[...].astype(BF16), preferred_element_type=F32)
    gb_ref[...] = proj(wb_ref).astype(gb_ref.dtype)
    gc = proj(wc_ref)
    u = proj(wu_ref)
    z_ref[...] = (gc * u).astype(z_ref.dtype)


def _conv_in(x, gain, w_in, layer, *, tm=1024, tn=512):
    t, d = x.shape
    nj = d // tn
    w_spec = lambda part: pl.BlockSpec((None, d, tn), lambda i, j: (layer, 0, part * nj + j))
    out_spec = pl.BlockSpec((tm, tn), lambda i, j: (i, j))
    return pl.pallas_call(
        _conv_in_kernel,
        grid=(t // tm, nj),
        in_specs=[pl.BlockSpec((tm, d), lambda i, j: (i, 0)),
                  pl.BlockSpec((1, d), lambda i, j: (0, 0)),
                  w_spec(0), w_spec(1), w_spec(2)],
        out_specs=[out_spec, out_spec],
        out_shape=[jax.ShapeDtypeStruct((t, d), BF16), jax.ShapeDtypeStruct((t, d), BF16)],
        scratch_shapes=[pltpu.VMEM((tm, d), BF16)],
        compiler_params=_cparams(2), name="conv_in",
    )(x, gain.reshape(1, d), w_in, w_in, w_in)


def _conv_out_kernel(gb_ref, z_ref, zprev_ref, cw_ref, w_ref, x_ref, o_ref, *, tiles_per_seq):
    i = pl.program_id(0)
    z = z_ref[...].astype(F32)
    prev = jnp.where(i % tiles_per_seq == 0, 0.0, zprev_ref[...].astype(F32))
    row = lax.broadcasted_iota(jnp.int32, z.shape, 0)
    z1 = jnp.where(row == 0, prev[SUBLANES - 1:SUBLANES], pltpu.roll(z, 1, 0))
    z2 = pltpu.roll(z, 2, 0)
    z2 = jnp.where(row == 0, prev[SUBLANES - 2:SUBLANES - 1], z2)
    z2 = jnp.where(row == 1, prev[SUBLANES - 1:SUBLANES], z2)
    cw = cw_ref[...]
    zc = cw[0:1] * z2 + cw[1:2] * z1 + cw[2:3] * z
    a = (gb_ref[...].astype(F32) * zc).astype(BF16)
    o_ref[...] = x_ref[...] + jnp.dot(a, w_ref[...], preferred_element_type=F32)


def _conv_out(gb, z, conv_w, w_out, layer, x, seq, *, tm=512):
    t, d = x.shape
    cw = jnp.zeros((SUBLANES, d), F32).at[:conv_w.shape[0]].set(conv_w)
    row_spec = pl.BlockSpec((tm, d), lambda i: (i, 0))
    prev_spec = pl.BlockSpec(
        (SUBLANES, d), lambda i: (jnp.maximum(i * (tm // SUBLANES) - 1, 0), 0))
    return pl.pallas_call(
        functools.partial(_conv_out_kernel, tiles_per_seq=seq // tm),
        grid=(t // tm,),
        in_specs=[row_spec, row_spec, prev_spec,
                  pl.BlockSpec((SUBLANES, d), lambda i: (0, 0)),
                  pl.BlockSpec((None, d, d), lambda i: (layer, 0, 0)), row_spec],
        out_specs=row_spec,
        out_shape=jax.ShapeDtypeStruct((t, d), F32),
        compiler_params=_cparams(1), name="conv_out",
    )(gb, z, z, cw, w_out, x)


def _router_kernel(x_ref, g_ref, r_ref, hn_ref, route_ref, gates_ref, counts_ref, carry_ref,
                   *, n_experts):
    i = pl.program_id(0)
    tm = x_ref.shape[0]

    @pl.when(i == 0)
    def _():
        carry_ref[...] = jnp.zeros(carry_ref.shape, F32)

    hn = _rms(x_ref[...], g_ref[...])
    hn_ref[...] = hn
    lane = lax.broadcasted_iota(jnp.int32, (tm, LANES), 1)
    logits = jnp.full((tm, LANES), -jnp.inf, F32)
    for e in range(n_experts):
        logit_e = jnp.sum(hn * r_ref[e:e + 1, :], axis=-1, keepdims=True)
        logits = jnp.where(lane == e, logit_e, logits)
    m1 = jnp.max(logits, axis=-1, keepdims=True)
    i1 = jnp.min(jnp.where(logits == m1, lane, LANES), axis=-1, keepdims=True)
    rest = jnp.where(lane == i1, -jnp.inf, logits)
    m2 = jnp.max(rest, axis=-1, keepdims=True)
    i2 = jnp.min(jnp.where(rest == m2, lane, LANES), axis=-1, keepdims=True)
    e2 = jnp.exp(m2 - m1)
    g1 = 1.0 / (1.0 + e2)
    g2 = e2 / (1.0 + e2)

    oh1 = (lane == i1).astype(F32)
    oh2 = (lane == i2).astype(F32)
    r_i = lax.broadcasted_iota(jnp.int32, (tm, tm), 0)
    c_i = lax.broadcasted_iota(jnp.int32, (tm, tm), 1)
    lower = (c_i < r_i).astype(BF16)
    before1 = jnp.dot(lower, oh1.astype(BF16), preferred_element_type=F32)
    before2 = jnp.dot(lower, oh2.astype(BF16), preferred_element_type=F32)
    tot1 = jnp.sum(oh1, axis=0, keepdims=True)
    tot2 = jnp.sum(oh2, axis=0, keepdims=True)
    carry = carry_ref[0:1]
    rank1 = jnp.sum(oh1 * (carry + before1), axis=-1, keepdims=True)
    rank2 = jnp.sum(oh2 * (carry + tot1 + before2), axis=-1, keepdims=True)
    new_carry = carry + tot1 + tot2
    carry_ref[...] = jnp.broadcast_to(new_carry, carry_ref.shape)
    counts_ref[...] = jnp.broadcast_to(new_carry, counts_ref.shape).astype(jnp.int32)

    route = jnp.where(lane == 0, i1, 0) + jnp.where(lane == 1, i2, 0)
    route += jnp.where(lane == 2, rank1.astype(jnp.int32), 0)
    route += jnp.where(lane == 3, rank2.astype(jnp.int32), 0)
    route_ref[...] = route
    gates_ref[...] = jnp.where(lane == 0, g1, 0.0) + jnp.where(lane == 1, g2, 0.0)


def _router(x, gain, router_w, *, tm=512):
    t, d = x.shape
    n_experts = router_w.shape[1]
    r_t = router_w.T
    row_spec = pl.BlockSpec((tm, d), lambda i: (i, 0))
    lane_spec = pl.BlockSpec((tm, LANES), lambda i: (i, 0))
    return pl.pallas_call(
        functools.partial(_router_kernel, n_experts=n_experts),
        grid=(t // tm,),
        in_specs=[row_spec, pl.BlockSpec((1, d), lambda i: (0, 0)),
                  pl.BlockSpec((n_experts, d), lambda i: (0, 0))],
        out_specs=[row_spec, lane_spec, lane_spec,
                   pl.BlockSpec((SUBLANES, LANES), lambda i: (0, 0))],
        out_shape=[jax.ShapeDtypeStruct((t, d), F32),
                   jax.ShapeDtypeStruct((t, LANES), jnp.int32),
                   jax.ShapeDtypeStruct((t, LANES), F32),
                   jax.ShapeDtypeStruct((SUBLANES, LANES), jnp.int32)],
        scratch_shapes=[pltpu.VMEM((SUBLANES, LANES), F32)],
        compiler_params=_cparams(1), name="moe_router",
    )(x, gain.reshape(1, d), r_t)


def _dispatch_kernel(pos1_ref, pos2_ref, hn_ref, xs_in_ref, xs_ref, sem):
    del xs_in_ref
    tm = hn_ref.shape[0]
    base = pl.program_id(0) * tm

    def row_copy(r, pos_ref):
        return pltpu.make_async_copy(
            hn_ref.at[pl.ds(r, 1)], xs_ref.at[pl.ds(pos_ref[base + r], 1)], sem)

    def issue(r, carry):
        row_copy(r, pos1_ref).start()
        row_copy(r, pos2_ref).start()
        return carry

    def drain(r, carry):
        row_copy(r, pos1_ref).wait()
        row_copy(r, pos2_ref).wait()
        return carry

    lax.fori_loop(0, tm, issue, 0, unroll=8)
    lax.fori_loop(0, tm, drain, 0, unroll=8)


def _dispatch(hn, pos1, pos2, xs0, *, tm=256):
    t, d = hn.shape
    n_rows = xs0.shape[0]
    grid_spec = pltpu.PrefetchScalarGridSpec(
        num_scalar_prefetch=2, grid=(t // tm,),
        in_specs=[pl.BlockSpec((tm, d), lambda i, p1, p2: (i, 0)),
                  pl.BlockSpec(memory_space=pl.ANY)],
        out_specs=pl.BlockSpec(memory_space=pl.ANY),
        scratch_shapes=[pltpu.SemaphoreType.DMA(())])
    return pl.pallas_call(
        _dispatch_kernel, grid_spec=grid_spec,
        out_shape=jax.ShapeDtypeStruct((n_rows, d), hn.dtype),
        input_output_aliases={3: 0},
        compiler_params=_cparams(1), name="moe_dispatch",
    )(pos1, pos2, hn, xs0)


EXPERT_ROW_STEP = 128


def _expert_kernel(te_ref, rows_ref, xs_ref, wg_ref, wu_ref, wd_ref, o_ref,
                   xstage_ref, xb_ref, sem):
    del te_ref
    i, f = pl.program_id(0), pl.program_id(1)
    n_tiles = pl.num_programs(0)
    tm = xb_ref.shape[0]
    rows = rows_ref[i]
    next_rows = rows_ref[jnp.minimum(i + 1, n_tiles - 1)]

    def tile_copy(tile):
        start = pl.multiple_of(tile * tm, SUBLANES)
        return pltpu.make_async_copy(xs_ref.at[pl.ds(start, tm)], xstage_ref, sem)

    @pl.when((f == 0) & (rows > 0))
    def _():
        @pl.when(i == 0)
        def _():
            tile_copy(0).start()
        tile_copy(i).wait()
        xb_ref[...] = xstage_ref[...].astype(BF16)

    @pl.when((f == 1) & (i + 1 < n_tiles) & (rows > 0) & (next_rows > 0))
    def _():
        tile_copy(i + 1).start()

    @pl.when(f == 0)
    def _():
        o_ref[...] = jnp.zeros(o_ref.shape, F32)

    for m in range(EXPERT_ROW_STEP, tm + 1, EXPERT_ROW_STEP):
        @pl.when(rows == m)
        def _():
            o_ref[0:m, :] += _swiglu_partial(xb_ref[0:m, :], wg_ref, wu_ref, wd_ref)


def _experts(xs, tile_expert, tile_rows, wg, wu, wd, layer, *, tm, tf=256):
    n_rows, d = xs.shape
    fe = wg.shape[3]
    nf = fe // tf
    assert nf >= 2
    f_eff = lambda i, f, rows: jnp.where(rows[i] > 0, f, nf - 1)
    grid_spec = pltpu.PrefetchScalarGridSpec(
        num_scalar_prefetch=2, grid=(n_rows // tm, nf),
        in_specs=[
            pl.BlockSpec(memory_space=pl.ANY),
            pl.BlockSpec((None, None, d, tf),
                         lambda i, f, te, rows: (layer, te[i], 0, f_eff(i, f, rows))),
            pl.BlockSpec((None, None, d, tf),
                         lambda i, f, te, rows: (layer, te[i], 0, f_eff(i, f, rows))),
            pl.BlockSpec((None, None, tf, d),
                         lambda i, f, te, rows: (layer, te[i], f_eff(i, f, rows), 0)),
        ],
        out_specs=pl.BlockSpec((tm, d), lambda i, f, te, rows: (i, 0)),
        scratch_shapes=[pltpu.VMEM((tm, d), F32), pltpu.VMEM((tm, d), BF16),
                        pltpu.SemaphoreType.DMA(())])
    return pl.pallas_call(
        _expert_kernel, grid_spec=grid_spec,
        out_shape=jax.ShapeDtypeStruct((n_rows, d), F32),
        compiler_params=_cparams(2), name="moe_experts",
    )(tile_expert, tile_rows, xs, wg, wu, wd)


def _combine_kernel(pos1_ref, pos2_ref, x_ref, gates_ref, ys_ref, o_ref, buf_ref, sems):
    tm = x_ref.shape[0]
    i = pl.program_id(0)
    n_steps = pl.num_programs(0)

    def row_copy(step, r, k, pos_ref):
        slot = step % 2
        return pltpu.make_async_copy(
            ys_ref.at[pl.ds(pos_ref[step * tm + r], 1)],
            buf_ref.at[slot, k, pl.ds(r, 1)], sems.at[slot])

    def issue_step(step):
        def issue(r, carry):
            row_copy(step, r, 0, pos1_ref).start()
            row_copy(step, r, 1, pos2_ref).start()
            return carry
        lax.fori_loop(0, tm, issue, 0, unroll=8)

    def drain_step(step):
        def drain(r, carry):
            row_copy(step, r, 0, pos1_ref).wait()
            row_copy(step, r, 1, pos2_ref).wait()
            return carry
        lax.fori_loop(0, tm, drain, 0, unroll=8)

    @pl.when(i == 0)
    def _():
        issue_step(0)

    @pl.when(i + 1 < n_steps)
    def _():
        issue_step(i + 1)

    drain_step(i)
    gates = gates_ref[...]
    slot = i % 2
    o_ref[...] = (x_ref[...] + gates[:, 0:1] * buf_ref[slot, 0]
                  + gates[:, 1:2] * buf_ref[slot, 1])


def _combine(x, gates, ys, pos1, pos2, *, tm=256):
    t, d = x.shape
    grid_spec = pltpu.PrefetchScalarGridSpec(
        num_scalar_prefetch=2, grid=(t // tm,),
        in_specs=[pl.BlockSpec((tm, d), lambda i, p1, p2: (i, 0)),
                  pl.BlockSpec((tm, LANES), lambda i, p1, p2: (i, 0)),
                  pl.BlockSpec(memory_space=pl.ANY)],
        out_specs=pl.BlockSpec((tm, d), lambda i, p1, p2: (i, 0)),
        scratch_shapes=[pltpu.VMEM((2, TOP_K, tm, d), F32), pltpu.SemaphoreType.DMA((2,))])
    return pl.pallas_call(
        _combine_kernel, grid_spec=grid_spec,
        out_shape=jax.ShapeDtypeStruct((t, d), F32),
        compiler_params=_cparams(1), name="moe_combine",
    )(pos1, pos2, x, gates, ys)


MOE_ROW_TILE = 1152


def _moe_rows(t, n_experts, tm_e=MOE_ROW_TILE):
    return ((t * TOP_K) // tm_e + n_experts) * tm_e


def _moe_layer(x, gain, router_w, wg, wu, wd, layer, xs_buffer, *, tm_e=MOE_ROW_TILE):
    t, d = x.shape
    n_experts = router_w.shape[1]
    hn, route, gates, counts = _router(x, gain, router_w)
    counts = counts[0, :n_experts]
    padded = (counts + tm_e - 1) // tm_e * tm_e
    ends = jnp.cumsum(padded)
    starts = ends - padded
    pos1 = starts[route[:, 0]] + route[:, 2]
    pos2 = starts[route[:, 1]] + route[:, 3]
    n_tiles = xs_buffer.shape[0] // tm_e
    n_valid = ends[-1] // tm_e
    tiles = jnp.arange(n_tiles, dtype=jnp.int32)
    tile_idx = jnp.minimum(tiles, jnp.maximum(n_valid - 1, 0))
    tile_expert = jnp.sum(tile_idx[:, None] * tm_e >= ends[None, :], axis=1).astype(jnp.int32)
    real_rows = jnp.clip(counts[tile_expert] - (tile_idx * tm_e - starts[tile_expert]), 0, tm_e)
    step = EXPERT_ROW_STEP
    tile_rows = jnp.where(tiles < n_valid, (real_rows + step - 1) // step * step, 0)
    xs = _dispatch(hn, pos1, pos2, xs_buffer)
    ys = _experts(xs, tile_expert, tile_rows.astype(jnp.int32), wg, wu, wd, layer, tm=tm_e)
    return _combine(x, gates, ys, pos1, pos2), xs


def kernel(x, attn_norm, attn_w_qkv, attn_q_gain, attn_k_gain, attn_lambda_q1, attn_lambda_k1,
           attn_lambda_q2, attn_lambda_k2, attn_subln_gain, attn_w_o,
           dense_norm, dense_w_gate, dense_w_up, dense_w_down,
           conv_norm, conv_w_in, conv_w, conv_w_out,
           moe_norm, moe_router, moe_w_gate, moe_w_up, moe_w_down):
    batch, seq, d = x.shape
    depth = attn_norm.shape[0] + conv_norm.shape[0]
    tables = _rope_tables(seq)
    h = x.reshape(batch * seq, d)
    w_o, w_out = attn_w_o.astype(BF16), conv_w_out.astype(BF16)
    xs_buffer = jnp.zeros((_moe_rows(batch * seq, moe_router.shape[2]), d), F32)
    for i in range(depth):
        j = i // 2
        if i % 2 == 0:
            lambda_init = 0.8 - 0.6 * math.exp(-0.3 * i)
            lam_params = jnp.zeros((SUBLANES, HEAD_DIM), F32).at[:4].set(jnp.stack(
                [attn_lambda_q1[j], attn_lambda_k1[j], attn_lambda_q2[j], attn_lambda_k2[j]]))
            qkv = _qkv_proj(h, attn_norm[j], attn_w_qkv, j)
            a = _diff_attention(qkv, lam_params, attn_subln_gain[j], attn_q_gain[j],
                                attn_k_gain[j], tables, batch, seq, lambda_init)
            h = _proj_residual(a, w_o, j, h)
            h = _dense_swiglu(h, dense_norm[j], dense_w_gate, dense_w_up, dense_w_down, j)
        else:
            gb, z = _conv_in(h, conv_norm[j], conv_w_in, j)
            h = _conv_out(gb, z, conv_w[j], w_out, j, h, seq)
            h, xs_buffer = _moe_layer(h, moe_norm[j], moe_router[j], moe_w_gate, moe_w_up,
                                      moe_w_down, j, xs_buffer)
    return h.reshape(batch, seq, d)
```

```python
import functools
import math

import jax
import jax.numpy as jnp
from jax import lax
from jax.experimental import pallas as pl
from jax.experimental.pallas import tpu as pltpu

F32 = jnp.float32
BF16 = jnp.bfloat16

HEAD_DIM = 128
ROT_DIM = HEAD_DIM // 4
ROPE_THETA = 500000.0
RMS_EPS = 1e-6
TOP_K = 2
LANES = 128
SUBLANES = 8
MXU_COLS = 256
VMEM_LIMIT = 56 * 1024 * 1024


def _cparams(n_axes):
    return pltpu.CompilerParams(
        dimension_semantics=("arbitrary",) * n_axes, vmem_limit_bytes=VMEM_LIMIT)


def _rms(x, gain):
    ms = jnp.mean(x * x, axis=-1, keepdims=True)
    return x * lax.rsqrt(ms + RMS_EPS) * gain


def _qkv_kernel(x_ref, g_ref, w_ref, o_ref, hn_ref):
    @pl.when(pl.program_id(1) == 0)
    def _():
        hn_ref[...] = _rms(x_ref[...], g_ref[...]).astype(BF16)

    o_ref[...] = jnp.dot(hn_ref[...], w_ref[...].astype(BF16),
                         preferred_element_type=F32).astype(o_ref.dtype)


def _qkv_proj(x, gain, w, layer, *, tm=1024, tn=512):
    t, d = x.shape
    n = w.shape[2]
    return pl.pallas_call(
        _qkv_kernel,
        grid=(t // tm, n // tn),
        in_specs=[pl.BlockSpec((tm, d), lambda i, j: (i, 0)),
                  pl.BlockSpec((1, d), lambda i, j: (0, 0)),
                  pl.BlockSpec((None, d, tn), lambda i, j: (layer, 0, j))],
        out_specs=pl.BlockSpec((tm, tn), lambda i, j: (i, j)),
        out_shape=jax.ShapeDtypeStruct((t, n), BF16),
        scratch_shapes=[pltpu.VMEM((tm, d), BF16)],
        compiler_params=_cparams(2), name="qkv_proj",
    )(x, gain.reshape(1, d), w)


def _rope_tables(seq):
    inv_freq = ROPE_THETA ** (-jnp.arange(0, ROT_DIM, 2, dtype=F32) / ROT_DIM)
    ang = inv_freq[:, None] * jnp.arange(seq, dtype=F32)[None, :]
    return jnp.cos(ang), jnp.sin(ang)


def _qk_norm_rope_t(x, gain_col, cos, sin):
    half = ROT_DIM // 2
    xt = x.astype(F32).T
    ms = jnp.mean(xt * xt, axis=0, keepdims=True)
    xn = xt * lax.rsqrt(ms + RMS_EPS) * gain_col
    x1, x2 = xn[0:half], xn[half:ROT_DIM]
    rot = jnp.concatenate([x1 * cos - x2 * sin, x2 * cos + x1 * sin, xn[ROT_DIM:]], axis=0)
    return rot.astype(BF16)


def _attn_kernel(q_ref, k_ref, v_ref, lam_ref, sg_ref, qg_ref, kg_ref, cos_ref, sin_ref,
                 o_ref, kt_ref, qt_ref, m_ref, l_ref, acc_ref, sa_ref, sb_ref, mba_ref, mbb_ref,
                 *, lambda_init):
    qi = pl.program_id(2)
    tq = q_ref.shape[0]
    tk = tq
    seq = k_ref.shape[0]
    to_log2 = HEAD_DIM ** -0.5 * math.log2(math.e)

    @pl.when(qi == 0)
    def _():
        def chunk(j, carry):
            r0 = pl.multiple_of(j * tk, tk)
            cos, sin = cos_ref[:, pl.ds(r0, tk)], sin_ref[:, pl.ds(r0, tk)]
            for c in range(2):
                kt_ref[c, :, pl.ds(r0, tk)] = _qk_norm_rope_t(
                    k_ref[pl.ds(r0, tk), c * HEAD_DIM:(c + 1) * HEAD_DIM], kg_ref[...], cos, sin)
            return carry
        lax.fori_loop(0, seq // tk, chunk, 0)

    q0 = pl.multiple_of(qi * tq, tq)
    cos_q, sin_q = cos_ref[:, pl.ds(q0, tq)], sin_ref[:, pl.ds(q0, tq)]
    for c in range(2):
        qt_ref[c] = _qk_norm_rope_t(
            q_ref[:, c * HEAD_DIM:(c + 1) * HEAD_DIM], qg_ref[...], cos_q, sin_q)
    m_ref[...] = jnp.full(m_ref.shape, -jnp.inf, F32)
    l_ref[...] = jnp.zeros(l_ref.shape, F32)
    acc_ref[...] = jnp.zeros(acc_ref.shape, F32)

    def scores(k0, s_ref, mb_ref):
        for c in range(2):
            st = lax.dot_general(kt_ref[c, :, pl.ds(k0, tk)], qt_ref[c],
                                 (((0,), (0,)), ((), ())),
                                 preferred_element_type=F32) * to_log2
            s_ref[c] = st
            mb_ref[c] = jnp.max(st, axis=0, keepdims=True)

    def consume(k0, s_ref, mb_ref, masked):
        v = v_ref[pl.ds(k0, tk), :]
        for c in range(2):
            st = s_ref[c]
            if masked:
                kpos = lax.broadcasted_iota(jnp.int32, (tk, tq), 0)
                qpos = lax.broadcasted_iota(jnp.int32, (tk, tq), 1)
                st = jnp.where(kpos <= qpos, st, -jnp.inf)
                m_blk = jnp.max(st, axis=0, keepdims=True)
            else:
                m_blk = mb_ref[c]
            m_prev = m_ref[c]
            m_new = jnp.maximum(m_prev, m_blk)
            alpha = jnp.exp2(m_prev - m_new)
            pt = jnp.exp2(st - m_new)
            l_ref[c] = alpha * l_ref[c] + jnp.sum(pt, axis=0, keepdims=True)
            pv = lax.dot_general(v, pt.astype(BF16), (((0,), (0,)), ((), ())),
                                 preferred_element_type=F32)
            acc_ref[c] = alpha * acc_ref[c] + pv
            m_ref[c] = m_new

    scores(0, sa_ref, mba_ref)

    def pair(p, carry):
        k0 = pl.multiple_of(2 * p * tk, tk)
        scores(k0 + tk, sb_ref, mbb_ref)
        consume(k0, sa_ref, mba_ref, False)
        scores(k0 + 2 * tk, sa_ref, mba_ref)
        consume(k0 + tk, sb_ref, mbb_ref, False)
        return carry

    lax.fori_loop(0, qi // 2, pair, 0)

    @pl.when(qi % 2 == 0)
    def _():
        consume(q0, sa_ref, mba_ref, True)

    @pl.when(qi % 2 == 1)
    def _():
        scores(q0, sb_ref, mbb_ref)
        consume(q0 - tk, sa_ref, mba_ref, False)
        consume(q0, sb_ref, mbb_ref, True)

    lp = lam_ref[...]
    lam = (jnp.exp(jnp.sum(lp[0:1] * lp[1:2], axis=-1, keepdims=True))
           - jnp.exp(jnp.sum(lp[2:3] * lp[3:4], axis=-1, keepdims=True))
           + lambda_init)
    ot = acc_ref[0] * (1.0 / l_ref[0]) - lam * (acc_ref[1] * (1.0 / l_ref[1]))
    ms = jnp.mean(ot * ot, axis=0, keepdims=True)
    on = ot * lax.rsqrt(ms + RMS_EPS) * sg_ref[...] * (1.0 - lambda_init)
    o_ref[...] = on.T.astype(o_ref.dtype)


def _diff_attention(qkv, lam_params, subln_gain, q_gain, k_gain, tables, batch, seq,
                    lambda_init, *, tq=512):
    t, n3 = qkv.shape
    d = n3 // 3
    vd = 2 * HEAD_DIM
    heads = d // vd
    nq = seq // tq
    const = lambda shape: pl.BlockSpec(shape, lambda b, h, qi: (0, 0))
    return pl.pallas_call(
        functools.partial(_attn_kernel, lambda_init=lambda_init),
        grid=(batch, heads, nq),
        in_specs=[
            pl.BlockSpec((tq, vd), lambda b, h, qi: (b * nq + qi, h)),
            pl.BlockSpec((seq, vd), lambda b, h, qi: (b, heads + h)),
            pl.BlockSpec((seq, vd), lambda b, h, qi: (b, 2 * heads + h)),
            const((SUBLANES, HEAD_DIM)), const((vd, 1)),
            const((HEAD_DIM, 1)), const((HEAD_DIM, 1)),
            const((ROT_DIM // 2, seq)), const((ROT_DIM // 2, seq)),
        ],
        out_specs=pl.BlockSpec((tq, vd), lambda b, h, qi: (b * nq + qi, h)),
        out_shape=jax.ShapeDtypeStruct((t, d), BF16),
        scratch_shapes=[pltpu.VMEM((2, HEAD_DIM, seq), BF16),
                        pltpu.VMEM((2, HEAD_DIM, tq), BF16),
                        pltpu.VMEM((2, 1, tq), F32), pltpu.VMEM((2, 1, tq), F32),
                        pltpu.VMEM((2, vd, tq), F32),
                        pltpu.VMEM((2, tq, tq), F32), pltpu.VMEM((2, tq, tq), F32),
                        pltpu.VMEM((2, 1, tq), F32), pltpu.VMEM((2, 1, tq), F32)],
        compiler_params=_cparams(3), name="diff_attention",
    )(qkv, qkv, qkv, lam_params, subln_gain.reshape(vd, 1),
      q_gain.reshape(HEAD_DIM, 1), k_gain.reshape(HEAD_DIM, 1), *tables)


def _proj_residual_kernel(a_ref, w_ref, x_ref, o_ref):
    o_ref[...] = x_ref[...] + jnp.dot(a_ref[...], w_ref[...], preferred_element_type=F32)


def _proj_residual(a, w, layer, x, *, tm=512):
    t, d = x.shape
    k = a.shape[1]
    return pl.pallas_call(
        _proj_residual_kernel,
        grid=(t // tm,),
        in_specs=[pl.BlockSpec((tm, k), lambda i: (i, 0)),
                  pl.BlockSpec((None, k, d), lambda i: (layer, 0, 0)),
                  pl.BlockSpec((tm, d), lambda i: (i, 0))],
        out_specs=pl.BlockSpec((tm, d), lambda i: (i, 0)),
        out_shape=jax.ShapeDtypeStruct((t, d), F32),
        compiler_params=_cparams(1), name="proj_residual",
    )(a, w, x)


def _swiglu_partial(h, wg_ref, wu_ref, wd_ref):
    gate = jnp.dot(h, wg_ref[...].astype(BF16), preferred_element_type=F32)
    up = jnp.dot(h, wu_ref[...].astype(BF16), preferred_element_type=F32)
    act = (gate * jax.nn.sigmoid(gate) * up).astype(BF16)
    return jnp.dot(act, wd_ref[...].astype(BF16), preferred_element_type=F32)


def _dense_kernel(x_hbm_ref, g_ref, wg_ref, wu_ref, wd_ref, o_ref, xstage_ref, hn_ref, sem):
    i, f = pl.program_id(0), pl.program_id(1)
    tm = hn_ref.shape[0]

    def tile_copy(tile):
        start = pl.multiple_of(tile * tm, SUBLANES)
        return pltpu.make_async_copy(x_hbm_ref.at[pl.ds(start, tm)], xstage_ref, sem)

    @pl.when(f == 0)
    def _():
        @pl.when(i == 0)
        def _():
            tile_copy(0).start()
        tile_copy(i).wait()
        x = xstage_ref[...]
        hn_ref[...] = _rms(x, g_ref[...]).astype(BF16)
        o_ref[...] = x

    @pl.when((f == 1) & (i + 1 < pl.num_programs(0)))
    def _():
        tile_copy(i + 1).start()

    o_ref[...] += _swiglu_partial(hn_ref[...], wg_ref, wu_ref, wd_ref)


def _dense_swiglu(x, gain, wg, wu, wd, layer, *, tm=1024, tf=256):
    t, d = x.shape
    ff = wg.shape[2]
    assert ff // tf >= 2
    return pl.pallas_call(
        _dense_kernel,
        grid=(t // tm, ff // tf),
        in_specs=[pl.BlockSpec(memory_space=pl.ANY),
                  pl.BlockSpec((1, d), lambda i, f: (0, 0)),
                  pl.BlockSpec((None, d, tf), lambda i, f: (layer, 0, f)),
                  pl.BlockSpec((None, d, tf), lambda i, f: (layer, 0, f)),
                  pl.BlockSpec((None, tf, d), lambda i, f: (layer, f, 0))],
        out_specs=pl.BlockSpec((tm, d), lambda i, f: (i, 0)),
        out_shape=jax.ShapeDtypeStruct((t, d), F32),
        scratch_shapes=[pltpu.VMEM((tm, d), F32), pltpu.VMEM((tm, d), BF16),
                        pltpu.SemaphoreType.DMA(())],
        compiler_params=_cparams(2), name="dense_swiglu",
    )(x, gain.reshape(1, d), wg, wu, wd)


def _conv_in_kernel(x_ref, g_ref, wb_ref, wc_ref, wu_ref, gb_ref, z_ref, hn_ref):
    j = pl.program_id(1)

    @pl.when(j == 0)
    def _():
        hn_ref[...] = _rms(x_ref[...], g_ref[...]).astype(BF16)

    h = hn_ref[...]
    proj = lambda w_ref: jnp.dot(h, w_ref[...].astype(BF16), preferred_element_type=F32)
    gb_ref[...] = proj(wb_ref).astype(gb_ref.dtype)
    gc = proj(wc_ref)
    u = proj(wu_ref)
    z_ref[...] = (gc * u).astype(z_ref.dtype)


def _conv_in(x, gain, w_in, layer, *, tm=1024, tn=512):
    t, d = x.shape
    nj = d // tn
    w_spec = lambda part: pl.BlockSpec((None, d, tn), lambda i, j: (layer, 0, part * nj + j))
    out_spec = pl.BlockSpec((tm, tn), lambda i, j: (i, j))
    return pl.pallas_call(
        _conv_in_kernel,
        grid=(t // tm, nj),
        in_specs=[pl.BlockSpec((tm, d), lambda i, j: (i, 0)),
                  pl.BlockSpec((1, d), lambda i, j: (0, 0)),
                  w_spec(0), w_spec(1), w_spec(2)],
        out_specs=[out_spec, out_spec],
        out_shape=[jax.ShapeDtypeStruct((t, d), BF16), jax.ShapeDtypeStruct((t, d), BF16)],
        scratch_shapes=[pltpu.VMEM((tm, d), BF16)],
        compiler_params=_cparams(2), name="conv_in",
    )(x, gain.reshape(1, d), w_in, w_in, w_in)


def _conv_out_kernel(gb_ref, z_ref, zprev_ref, cw_ref, w_ref, x_ref, o_ref, *, tiles_per_seq):
    i = pl.program_id(0)
    z = z_ref[...].astype(F32)
    prev = jnp.where(i % tiles_per_seq == 0, 0.0, zprev_ref[...].astype(F32))
    row = lax.broadcasted_iota(jnp.int32, z.shape, 0)
    z1 = jnp.where(row == 0, prev[SUBLANES - 1:SUBLANES], pltpu.roll(z, 1, 0))
    z2 = pltpu.roll(z, 2, 0)
    z2 = jnp.where(row == 0, prev[SUBLANES - 2:SUBLANES - 1], z2)
    z2 = jnp.where(row == 1, prev[SUBLANES - 1:SUBLANES], z2)
    cw = cw_ref[...]
    zc = cw[0:1] * z2 + cw[1:2] * z1 + cw[2:3] * z
    a = (gb_ref[...].astype(F32) * zc).astype(BF16)
    o_ref[...] = x_ref[...] + jnp.dot(a, w_ref[...], preferred_element_type=F32)


def _conv_out(gb, z, conv_w, w_out, layer, x, seq, *, tm=512):
    t, d = x.shape
    cw = jnp.zeros((SUBLANES, d), F32).at[:conv_w.shape[0]].set(conv_w)
    row_spec = pl.BlockSpec((tm, d), lambda i: (i, 0))
    prev_spec = pl.BlockSpec(
        (SUBLANES, d), lambda i: (jnp.maximum(i * (tm // SUBLANES) - 1, 0), 0))
    return pl.pallas_call(
        functools.partial(_conv_out_kernel, tiles_per_seq=seq // tm),
        grid=(t // tm,),
        in_specs=[row_spec, row_spec, prev_spec,
                  pl.BlockSpec((SUBLANES, d), lambda i: (0, 0)),
                  pl.BlockSpec((None, d, d), lambda i: (layer, 0, 0)), row_spec],
        out_specs=row_spec,
        out_shape=jax.ShapeDtypeStruct((t, d), F32),
        compiler_params=_cparams(1), name="conv_out",
    )(gb, z, z, cw, w_out, x)


def _router_kernel(x_ref, g_ref, r_ref, hn_ref, route_ref, gates_ref, counts_ref, carry_ref,
                   *, n_experts):
    i = pl.program_id(0)
    tm = x_ref.shape[0]

    @pl.when(i == 0)
    def _():
        carry_ref[...] = jnp.zeros(carry_ref.shape, F32)

    hn = _rms(x_ref[...], g_ref[...])
    hn_ref[...] = hn
    lane = lax.broadcasted_iota(jnp.int32, (tm, LANES), 1)
    logits = jnp.full((tm, LANES), -jnp.inf, F32)
    for e in range(n_experts):
        logit_e = jnp.sum(hn * r_ref[e:e + 1, :], axis=-1, keepdims=True)
        logits = jnp.where(lane == e, logit_e, logits)
    m1 = jnp.max(logits, axis=-1, keepdims=True)
    i1 = jnp.min(jnp.where(logits == m1, lane, LANES), axis=-1, keepdims=True)
    rest = jnp.where(lane == i1, -jnp.inf, logits)
    m2 = jnp.max(rest, axis=-1, keepdims=True)
    i2 = jnp.min(jnp.where(rest == m2, lane, LANES), axis=-1, keepdims=True)
    e2 = jnp.exp(m2 - m1)
    g1 = 1.0 / (1.0 + e2)
    g2 = e2 / (1.0 + e2)

    oh1 = (lane == i1).astype(F32)
    oh2 = (lane == i2).astype(F32)
    r_i = lax.broadcasted_iota(jnp.int32, (tm, tm), 0)
    c_i = lax.broadcasted_iota(jnp.int32, (tm, tm), 1)
    lower = (c_i < r_i).astype(BF16)
    before1 = jnp.dot(lower, oh1.astype(BF16), preferred_element_type=F32)
    before2 = jnp.dot(lower, oh2.astype(BF16), preferred_element_type=F32)
    tot1 = jnp.sum(oh1, axis=0, keepdims=True)
    tot2 = jnp.sum(oh2, axis=0, keepdims=True)
    carry = carry_ref[0:1]
    rank1 = jnp.sum(oh1 * (carry + before1), axis=-1, keepdims=True)
    rank2 = jnp.sum(oh2 * (carry + tot1 + before2), axis=-1, keepdims=True)
    new_carry = carry + tot1 + tot2
    carry_ref[...] = jnp.broadcast_to(new_carry, carry_ref.shape)
    counts_ref[...] = jnp.broadcast_to(new_carry, counts_ref.shape).astype(jnp.int32)

    route = jnp.where(lane == 0, i1, 0) + jnp.where(lane == 1, i2, 0)
    route += jnp.where(lane == 2, rank1.astype(jnp.int32), 0)
    route += jnp.where(lane == 3, rank2.astype(jnp.int32), 0)
    route_ref[...] = route
    gates_ref[...] = jnp.where(lane == 0, g1, 0.0) + jnp.where(lane == 1, g2, 0.0)


def _router(x, gain, router_w, *, tm=512):
    t, d = x.shape
    n_experts = router_w.shape[1]
    r_t = router_w.T
    row_spec = pl.BlockSpec((tm, d), lambda i: (i, 0))
    lane_spec = pl.BlockSpec((tm, LANES), lambda i: (i, 0))
    return pl.pallas_call(
        functools.partial(_router_kernel, n_experts=n_experts),
        grid=(t // tm,),
        in_specs=[row_spec, pl.BlockSpec((1, d), lambda i: (0, 0)),
                  pl.BlockSpec((n_experts, d), lambda i: (0, 0))],
        out_specs=[row_spec, lane_spec, lane_spec,
                   pl.BlockSpec((SUBLANES, LANES), lambda i: (0, 0))],
        out_shape=[jax.ShapeDtypeStruct((t, d), F32),
                   jax.ShapeDtypeStruct((t, LANES), jnp.int32),
                   jax.ShapeDtypeStruct((t, LANES), F32),
                   jax.ShapeDtypeStruct((SUBLANES, LANES), jnp.int32)],
        scratch_shapes=[pltpu.VMEM((SUBLANES, LANES), F32)],
        compiler_params=_cparams(1), name="moe_router",
    )(x, gain.reshape(1, d), r_t)


def _dispatch_kernel(pos1_ref, pos2_ref, hn_ref, xs_in_ref, xs_ref, sem):
    del xs_in_ref
    tm = hn_ref.shape[0]
    base = pl.program_id(0) * tm

    def row_copy(r, pos_ref):
        return pltpu.make_async_copy(
            hn_ref.at[pl.ds(r, 1)], xs_ref.at[pl.ds(pos_ref[base + r], 1)], sem)

    def issue(r, carry):
        row_copy(r, pos1_ref).start(priority=0)
        row_copy(r, pos2_ref).start(priority=1)
        return carry

    def drain(r, carry):
        row_copy(r, pos1_ref).wait()
        row_copy(r, pos2_ref).wait()
        return carry

    lax.fori_loop(0, tm, issue, 0, unroll=8)
    lax.fori_loop(0, tm, drain, 0, unroll=8)


def _dispatch(hn, pos1, pos2, xs0, *, tm=256):
    t, d = hn.shape
    n_rows = xs0.shape[0]
    grid_spec = pltpu.PrefetchScalarGridSpec(
        num_scalar_prefetch=2, grid=(t // tm,),
        in_specs=[pl.BlockSpec((tm, d), lambda i, p1, p2: (i, 0)),
                  pl.BlockSpec(memory_space=pl.ANY)],
        out_specs=pl.BlockSpec(memory_space=pl.ANY),
        scratch_shapes=[pltpu.SemaphoreType.DMA(())])
    return pl.pallas_call(
        _dispatch_kernel, grid_spec=grid_spec,
        out_shape=jax.ShapeDtypeStruct((n_rows, d), hn.dtype),
        input_output_aliases={3: 0},
        compiler_params=_cparams(1), name="moe_dispatch",
    )(pos1, pos2, hn, xs0)


EXPERT_ROW_STEP = 128


def _expert_kernel(te_ref, rows_ref, xs_ref, wg_ref, wu_ref, wd_ref, o_ref,
                   xstage_ref, xb_ref, sem):
    del te_ref
    i, f = pl.program_id(0), pl.program_id(1)
    n_tiles = pl.num_programs(0)
    tm = xb_ref.shape[0]
    rows = rows_ref[i]
    next_rows = rows_ref[jnp.minimum(i + 1, n_tiles - 1)]

    def tile_copy(tile):
        start = pl.multiple_of(tile * tm, SUBLANES)
        return pltpu.make_async_copy(xs_ref.at[pl.ds(start, tm)], xstage_ref, sem)

    @pl.when((f == 0) & (rows > 0))
    def _():
        @pl.when(i == 0)
        def _():
            tile_copy(0).start()
        tile_copy(i).wait()
        xb_ref[...] = xstage_ref[...].astype(BF16)

    @pl.when((f == 1) & (i + 1 < n_tiles) & (rows > 0) & (next_rows > 0))
    def _():
        tile_copy(i + 1).start()

    @pl.when(f == 0)
    def _():
        o_ref[...] = jnp.zeros(o_ref.shape, F32)

    for m in range(EXPERT_ROW_STEP, tm + 1, EXPERT_ROW_STEP):
        @pl.when(rows == m)
        def _():
            o_ref[0:m, :] += _swiglu_partial(xb_ref[0:m, :], wg_ref, wu_ref, wd_ref)


def _experts(xs, tile_expert, tile_rows, wg, wu, wd, layer, *, tm, tf=256):
    n_rows, d = xs.shape
    fe = wg.shape[3]
    nf = fe // tf
    assert nf >= 2
    f_eff = lambda i, f, rows: jnp.where(rows[i] > 0, f, nf - 1)
    grid_spec = pltpu.PrefetchScalarGridSpec(
        num_scalar_prefetch=2, grid=(n_rows // tm, nf),
        in_specs=[
            pl.BlockSpec(memory_space=pl.ANY),
            pl.BlockSpec((None, None, d, tf),
                         lambda i, f, te, rows: (layer, te[i], 0, f_eff(i, f, rows))),
            pl.BlockSpec((None, None, d, tf),
                         lambda i, f, te, rows: (layer, te[i], 0, f_eff(i, f, rows))),
            pl.BlockSpec((None, None, tf, d),
                         lambda i, f, te, rows: (layer, te[i], f_eff(i, f, rows), 0)),
        ],
        out_specs=pl.BlockSpec((tm, d), lambda i, f, te, rows: (i, 0)),
        scratch_shapes=[pltpu.VMEM((tm, d), F32), pltpu.VMEM((tm, d), BF16),
                        pltpu.SemaphoreType.DMA(())])
    return pl.pallas_call(
        _expert_kernel, grid_spec=grid_spec,
        out_shape=jax.ShapeDtypeStruct((n_rows, d), F32),
        compiler_params=_cparams(2), name="moe_experts",
    )(tile_expert, tile_rows, xs, wg, wu, wd)


def _combine_kernel(pos1_ref, pos2_ref, x_ref, gates_ref, ys_ref, o_ref, buf_ref, sems):
    tm = x_ref.shape[0]
    i = pl.program_id(0)
    n_steps = pl.num_programs(0)

    def row_copy(step, r, k, pos_ref):
        slot = step % 2
        return pltpu.make_async_copy(
            ys_ref.at[pl.ds(pos_ref[step * tm + r], 1)],
            buf_ref.at[slot, k, pl.ds(r, 1)], sems.at[slot])

    def issue_step(step):
        def issue(r, carry):
            row_copy(step, r, 0, pos1_ref).start(priority=0)
            row_copy(step, r, 1, pos2_ref).start(priority=1)
            return carry
        lax.fori_loop(0, tm, issue, 0, unroll=8)

    def drain_step(step):
        def drain(r, carry):
            row_copy(step, r, 0, pos1_ref).wait()
            row_copy(step, r, 1, pos2_ref).wait()
            return carry
        lax.fori_loop(0, tm, drain, 0, unroll=8)

    @pl.when(i == 0)
    def _():
        issue_step(0)

    @pl.when(i + 1 < n_steps)
    def _():
        issue_step(i + 1)

    drain_step(i)
    gates = gates_ref[...]
    slot = i % 2
    o_ref[...] = (x_ref[...] + gates[:, 0:1] * buf_ref[slot, 0]
                  + gates[:, 1:2] * buf_ref[slot, 1])


def _combine(x, gates, ys, pos1, pos2, *, tm=256):
    t, d = x.shape
    grid_spec = pltpu.PrefetchScalarGridSpec(
        num_scalar_prefetch=2, grid=(t // tm,),
        in_specs=[pl.BlockSpec((tm, d), lambda i, p1, p2: (i, 0)),
                  pl.BlockSpec((tm, LANES), lambda i, p1, p2: (i, 0)),
                  pl.BlockSpec(memory_space=pl.ANY)],
        out_specs=pl.BlockSpec((tm, d), lambda i, p1, p2: (i, 0)),
        scratch_shapes=[pltpu.VMEM((2, TOP_K, tm, d), F32), pltpu.SemaphoreType.DMA((2,))])
    return pl.pallas_call(
        _combine_kernel, grid_spec=grid_spec,
        out_shape=jax.ShapeDtypeStruct((t, d), F32),
        compiler_params=_cparams(1), name="moe_combine",
    )(pos1, pos2, x, gates, ys)


MOE_ROW_TILE = 1152


def _moe_rows(t, n_experts, tm_e=MOE_ROW_TILE):
    return ((t * TOP_K) // tm_e + n_experts) * tm_e


def _moe_layer(x, gain, router_w, wg, wu, wd, layer, xs_buffer, *, tm_e=MOE_ROW_TILE):
    t, d = x.shape
    n_experts = router_w.shape[1]
    hn, route, gates, counts = _router(x, gain, router_w)
    counts = counts[0, :n_experts]
    padded = (counts + tm_e - 1) // tm_e * tm_e
    ends = jnp.cumsum(padded)
    starts = ends - padded
    pos1 = starts[route[:, 0]] + route[:, 2]
    pos2 = starts[route[:, 1]] + route[:, 3]
    n_tiles = xs_buffer.shape[0] // tm_e
    n_valid = ends[-1] // tm_e
    tiles = jnp.arange(n_tiles, dtype=jnp.int32)
    tile_idx = jnp.minimum(tiles, jnp.maximum(n_valid - 1, 0))
    tile_expert = jnp.sum(tile_idx[:, None] * tm_e >= ends[None, :], axis=1).astype(jnp.int32)
    real_rows = jnp.clip(counts[tile_expert] - (tile_idx * tm_e - starts[tile_expert]), 0, tm_e)
    step = EXPERT_ROW_STEP
    tile_rows = jnp.where(tiles < n_valid, (real_rows + step - 1) // step * step, 0)
    xs = _dispatch(hn, pos1, pos2, xs_buffer)
    ys = _experts(xs, tile_expert, tile_rows.astype(jnp.int32), wg, wu, wd, layer, tm=tm_e)
    return _combine(x, gates, ys, pos1, pos2), xs


def kernel(x, attn_norm, attn_w_qkv, attn_q_gain, attn_k_gain, attn_lambda_q1, attn_lambda_k1,
           attn_lambda_q2, attn_lambda_k2, attn_subln_gain, attn_w_o,
           dense_norm, dense_w_gate, dense_w_up, dense_w_down,
           conv_norm, conv_w_in, conv_w, conv_w_out,
           moe_norm, moe_router, moe_w_gate, moe_w_up, moe_w_down):
    batch, seq, d = x.shape
    depth = attn_norm.shape[0] + conv_norm.shape[0]
    tables = _rope_tables(seq)
    h = x.reshape(batch * seq, d)
    w_o, w_out = attn_w_o.astype(BF16), conv_w_out.astype(BF16)
    xs_buffer = jnp.zeros((_moe_rows(batch * seq, moe_router.shape[2]), d), F32)
    for i in range(depth):
        j = i // 2
        if i % 2 == 0:
            lambda_init = 0.8 - 0.6 * math.exp(-0.3 * i)
            lam_params = jnp.zeros((SUBLANES, HEAD_DIM), F32).at[:4].set(jnp.stack(
                [attn_lambda_q1[j], attn_lambda_k1[j], attn_lambda_q2[j], attn_lambda_k2[j]]))
            qkv = _qkv_proj(h, attn_norm[j], attn_w_qkv, j)
            a = _diff_attention(qkv, lam_params, attn_subln_gain[j], attn_q_gain[j],
                                attn_k_gain[j], tables, batch, seq, lambda_init)
            h = _proj_residual(a, w_o, j, h)
            h = _dense_swiglu(h, dense_norm[j], dense_w_gate, dense_w_up, dense_w_down, j)
        else:
            gb, z = _conv_in(h, conv_norm[j], conv_w_in, j)
            h = _conv_out(gb, z, conv_w[j], w_out, j, h, seq)
            h, xs_buffer = _moe_layer(h, moe_norm[j], moe_router[j], moe_w_gate, moe_w_up,
                                      moe_w_down, j, xs_buffer)
    return h.reshape(batch, seq, d)
```

```python
import functools
import math

import jax
import jax.numpy as jnp
from jax import lax
from jax.experimental import pallas as pl
from jax.experimental.pallas import tpu as pltpu

F32 = jnp.float32
BF16 = jnp.bfloat16

HEAD_DIM = 128
ROT_DIM = HEAD_DIM // 4
ROPE_THETA = 500000.0
RMS_EPS = 1e-6
TOP_K = 2
LANES = 128
SUBLANES = 8
MXU_COLS = 256
VMEM_LIMIT = 56 * 1024 * 1024


def _cparams(n_axes):
    return pltpu.CompilerParams(
        dimension_semantics=("arbitrary",) * n_axes, vmem_limit_bytes=VMEM_LIMIT)


def _rms(x, gain):
    ms = jnp.mean(x * x, axis=-1, keepdims=True)
    return x * lax.rsqrt(ms + RMS_EPS) * gain


def _qkv_kernel(x_ref, g_ref, w_ref, o_ref, hn_ref):
    @pl.when(pl.program_id(1) == 0)
    def _():
        hn_ref[...] = _rms(x_ref[...], g_ref[...]).astype(BF16)

    o_ref[...] = jnp.dot(hn_ref[...], w_ref[...].astype(BF16),
                         preferred_element_type=F32).astype(o_ref.dtype)


def _qkv_proj(x, gain, w, layer, *, tm=1024, tn=1024):
    t, d = x.shape
    n = w.shape[2]
    return pl.pallas_call(
        _qkv_kernel,
        grid=(t // tm, n // tn),
        in_specs=[pl.BlockSpec((tm, d), lambda i, j: (i, 0)),
                  pl.BlockSpec((1, d), lambda i, j: (0, 0)),
                  pl.BlockSpec((None, d, tn), lambda i, j: (layer, 0, j))],
        out_specs=pl.BlockSpec((tm, tn), lambda i, j: (i, j)),
        out_shape=jax.ShapeDtypeStruct((t, n), BF16),
        scratch_shapes=[pltpu.VMEM((tm, d), BF16)],
        compiler_params=_cparams(2), name="qkv_proj",
    )(x, gain.reshape(1, d), w)


def _rope_tables(seq):
    inv_freq = ROPE_THETA ** (-jnp.arange(0, ROT_DIM, 2, dtype=F32) / ROT_DIM)
    ang = inv_freq[:, None] * jnp.arange(seq, dtype=F32)[None, :]
    return jnp.cos(ang), jnp.sin(ang)


def _qk_norm_rope_t(x, gain_col, cos, sin):
    half = ROT_DIM // 2
    xt = x.astype(F32).T
    ms = jnp.mean(xt * xt, axis=0, keepdims=True)
    xn = xt * lax.rsqrt(ms + RMS_EPS) * gain_col
    x1, x2 = xn[0:half], xn[half:ROT_DIM]
    rot = jnp.concatenate([x1 * cos - x2 * sin, x2 * cos + x1 * sin, xn[ROT_DIM:]], axis=0)
    return rot.astype(BF16)


def _attn_kernel(q_ref, k_ref, v_ref, lam_ref, sg_ref, qg_ref, kg_ref, cos_ref, sin_ref,
                 o_ref, kt_ref, qt_ref, m_ref, l_ref, acc_ref, sa_ref, sb_ref, mba_ref, mbb_ref,
                 *, lambda_init):
    qi = pl.program_id(2)
    tq = q_ref.shape[0]
    tk = tq
    seq = k_ref.shape[0]
    to_log2 = HEAD_DIM ** -0.5 * math.log2(math.e)

    @pl.when(qi == 0)
    def _():
        def chunk(j, carry):
            r0 = pl.multiple_of(j * tk, tk)
            cos, sin = cos_ref[:, pl.ds(r0, tk)], sin_ref[:, pl.ds(r0, tk)]
            for c in range(2):
                kt_ref[c, :, pl.ds(r0, tk)] = _qk_norm_rope_t(
                    k_ref[pl.ds(r0, tk), c * HEAD_DIM:(c + 1) * HEAD_DIM], kg_ref[...], cos, sin)
            return carry
        lax.fori_loop(0, seq // tk, chunk, 0)

    q0 = pl.multiple_of(qi * tq, tq)
    cos_q, sin_q = cos_ref[:, pl.ds(q0, tq)], sin_ref[:, pl.ds(q0, tq)]
    for c in range(2):
        qt_ref[c] = _qk_norm_rope_t(
            q_ref[:, c * HEAD_DIM:(c + 1) * HEAD_DIM], qg_ref[...], cos_q, sin_q)
    m_ref[...] = jnp.full(m_ref.shape, -jnp.inf, F32)
    l_ref[...] = jnp.zeros(l_ref.shape, F32)
    acc_ref[...] = jnp.zeros(acc_ref.shape, F32)

    def scores(k0, s_ref, mb_ref):
        for c in range(2):
            st = lax.dot_general(kt_ref[c, :, pl.ds(k0, tk)], qt_ref[c],
                                 (((0,), (0,)), ((), ())),
                                 preferred_element_type=F32) * to_log2
            s_ref[c] = st
            mb_ref[c] = jnp.max(st, axis=0, keepdims=True)

    def consume(k0, s_ref, mb_ref, masked):
        v = v_ref[pl.ds(k0, tk), :]
        for c in range(2):
            st = s_ref[c]
            if masked:
                kpos = lax.broadcasted_iota(jnp.int32, (tk, tq), 0)
                qpos = lax.broadcasted_iota(jnp.int32, (tk, tq), 1)
                st = jnp.where(kpos <= qpos, st, -jnp.inf)
                m_blk = jnp.max(st, axis=0, keepdims=True)
            else:
                m_blk = mb_ref[c]
            m_prev = m_ref[c]
            m_new = jnp.maximum(m_prev, m_blk)
            alpha = jnp.exp2(m_prev - m_new)
            pt = jnp.exp2(st - m_new)
            l_ref[c] = alpha * l_ref[c] + jnp.sum(pt, axis=0, keepdims=True)
            pv = lax.dot_general(v, pt.astype(BF16), (((0,), (0,)), ((), ())),
                                 preferred_element_type=F32)
            acc_ref[c] = alpha * acc_ref[c] + pv
            m_ref[c] = m_new

    scores(0, sa_ref, mba_ref)

    def pair(p, carry):
        k0 = pl.multiple_of(2 * p * tk, tk)
        scores(k0 + tk, sb_ref, mbb_ref)
        consume(k0, sa_ref, mba_ref, False)
        scores(k0 + 2 * tk, sa_ref, mba_ref)
        consume(k0 + tk, sb_ref, mbb_ref, False)
        return carry

    lax.fori_loop(0, qi // 2, pair, 0)

    @pl.when(qi % 2 == 0)
    def _():
        consume(q0, sa_ref, mba_ref, True)

    @pl.when(qi % 2 == 1)
    def _():
        scores(q0, sb_ref, mbb_ref)
        consume(q0 - tk, sa_ref, mba_ref, False)
        consume(q0, sb_ref, mbb_ref, True)

    lp = lam_ref[...]
    lam = (jnp.exp(jnp.sum(lp[0:1] * lp[1:2], axis=-1, keepdims=True))
           - jnp.exp(jnp.sum(lp[2:3] * lp[3:4], axis=-1, keepdims=True))
           + lambda_init)
    ot = acc_ref[0] * (1.0 / l_ref[0]) - lam * (acc_ref[1] * (1.0 / l_ref[1]))
    ms = jnp.mean(ot * ot, axis=0, keepdims=True)
    on = ot * lax.rsqrt(ms + RMS_EPS) * sg_ref[...] * (1.0 - lambda_init)
    o_ref[...] = on.T.astype(o_ref.dtype)


def _diff_attention(qkv, lam_params, subln_gain, q_gain, k_gain, tables, batch, seq,
                    lambda_init, *, tq=512):
    t, n3 = qkv.shape
    d = n3 // 3
    vd = 2 * HEAD_DIM
    heads = d // vd
    nq = seq // tq
    const = lambda shape: pl.BlockSpec(shape, lambda b, h, qi: (0, 0))
    return pl.pallas_call(
        functools.partial(_attn_kernel, lambda_init=lambda_init),
        grid=(batch, heads, nq),
        in_specs=[
            pl.BlockSpec((tq, vd), lambda b, h, qi: (b * nq + qi, h)),
            pl.BlockSpec((seq, vd), lambda b, h, qi: (b, heads + h)),
            pl.BlockSpec((seq, vd), lambda b, h, qi: (b, 2 * heads + h)),
            const((SUBLANES, HEAD_DIM)), const((vd, 1)),
            const((HEAD_DIM, 1)), const((HEAD_DIM, 1)),
            const((ROT_DIM // 2, seq)), const((ROT_DIM // 2, seq)),
        ],
        out_specs=pl.BlockSpec((tq, vd), lambda b, h, qi: (b * nq + qi, h)),
        out_shape=jax.ShapeDtypeStruct((t, d), BF16),
        scratch_shapes=[pltpu.VMEM((2, HEAD_DIM, seq), BF16),
                        pltpu.VMEM((2, HEAD_DIM, tq), BF16),
                        pltpu.VMEM((2, 1, tq), F32), pltpu.VMEM((2, 1, tq), F32),
                        pltpu.VMEM((2, vd, tq), F32),
                        pltpu.VMEM((2, tq, tq), F32), pltpu.VMEM((2, tq, tq), F32),
                        pltpu.VMEM((2, 1, tq), F32), pltpu.VMEM((2, 1, tq), F32)],
        compiler_params=_cparams(3), name="diff_attention",
    )(qkv, qkv, qkv, lam_params, subln_gain.reshape(vd, 1),
      q_gain.reshape(HEAD_DIM, 1), k_gain.reshape(HEAD_DIM, 1), *tables)


def _proj_residual_kernel(a_ref, w_ref, x_ref, o_ref):
    o_ref[...] = x_ref[...] + jnp.dot(a_ref[...], w_ref[...], preferred_element_type=F32)


def _proj_residual(a, w, layer, x, *, tm=512):
    t, d = x.shape
    k = a.shape[1]
    return pl.pallas_call(
        _proj_residual_kernel,
        grid=(t // tm,),
        in_specs=[pl.BlockSpec((tm, k), lambda i: (i, 0)),
                  pl.BlockSpec((None, k, d), lambda i: (layer, 0, 0)),
                  pl.BlockSpec((tm, d), lambda i: (i, 0))],
        out_specs=pl.BlockSpec((tm, d), lambda i: (i, 0)),
        out_shape=jax.ShapeDtypeStruct((t, d), F32),
        compiler_params=_cparams(1), name="proj_residual",
    )(a, w, x)


def _swiglu_partial(h, wg_ref, wu_ref, wd_ref):
    gate = jnp.dot(h, wg_ref[...].astype(BF16), preferred_element_type=F32)
    up = jnp.dot(h, wu_ref[...].astype(BF16), preferred_element_type=F32)
    act = (gate * jax.nn.sigmoid(gate) * up).astype(BF16)
    return jnp.dot(act, wd_ref[...].astype(BF16), preferred_element_type=F32)


def _dense_kernel(x_hbm_ref, g_ref, wg_ref, wu_ref, wd_ref, o_ref, xstage_ref, hn_ref, sem):
    i, f = pl.program_id(0), pl.program_id(1)
    tm = hn_ref.shape[0]

    def tile_copy(tile):
        start = pl.multiple_of(tile * tm, SUBLANES)
        return pltpu.make_async_copy(x_hbm_ref.at[pl.ds(start, tm)], xstage_ref, sem)

    @pl.when(f == 0)
    def _():
        @pl.when(i == 0)
        def _():
            tile_copy(0).start()
        tile_copy(i).wait()
        x = xstage_ref[...]
        hn_ref[...] = _rms(x, g_ref[...]).astype(BF16)
        o_ref[...] = x

    @pl.when((f == 1) & (i + 1 < pl.num_programs(0)))
    def _():
        tile_copy(i + 1).start()

    o_ref[...] += _swiglu_partial(hn_ref[...], wg_ref, wu_ref, wd_ref)


def _dense_swiglu(x, gain, wg, wu, wd, layer, *, tm=1024, tf=256):
    t, d = x.shape
    ff = wg.shape[2]
    assert ff // tf >= 2
    return pl.pallas_call(
        _dense_kernel,
        grid=(t // tm, ff // tf),
        in_specs=[pl.BlockSpec(memory_space=pl.ANY),
                  pl.BlockSpec((1, d), lambda i, f: (0, 0)),
                  pl.BlockSpec((None, d, tf), lambda i, f: (layer, 0, f)),
                  pl.BlockSpec((None, d, tf), lambda i, f: (layer, 0, f)),
                  pl.BlockSpec((None, tf, d), lambda i, f: (layer, f, 0))],
        out_specs=pl.BlockSpec((tm, d), lambda i, f: (i, 0)),
        out_shape=jax.ShapeDtypeStruct((t, d), F32),
        scratch_shapes=[pltpu.VMEM((tm, d), F32), pltpu.VMEM((tm, d), BF16),
                        pltpu.SemaphoreType.DMA(())],
        compiler_params=_cparams(2), name="dense_swiglu",
    )(x, gain.reshape(1, d), wg, wu, wd)


def _conv_in_kernel(x_ref, g_ref, wb_ref, wc_ref, wu_ref, gb_ref, z_ref, hn_ref):
    j = pl.program_id(1)

    @pl.when(j == 0)
    def _():
        hn_ref[...] = _rms(x_ref[...], g_ref[...]).astype(BF16)

    h = hn_ref[...]
    proj = lambda w_ref: jnp.dot(h, w_ref[...].astype(BF16), preferred_element_type=F32)
    gb_ref[...] = proj(wb_ref).astype(gb_ref.dtype)
    gc = proj(wc_ref)
    u = proj(wu_ref)
    z_ref[...] = (gc * u).astype(z_ref.dtype)


def _conv_in(x, gain, w_in, layer, *, tm=1024, tn=512):
    t, d = x.shape
    nj = d // tn
    w_spec = lambda part: pl.BlockSpec((None, d, tn), lambda i, j: (layer, 0, part * nj + j))
    out_spec = pl.BlockSpec((tm, tn), lambda i, j: (i, j))
    return pl.pallas_call(
        _conv_in_kernel,
        grid=(t // tm, nj),
        in_specs=[pl.BlockSpec((tm, d), lambda i, j: (i, 0)),
                  pl.BlockSpec((1, d), lambda i, j: (0, 0)),
                  w_spec(0), w_spec(1), w_spec(2)],
        out_specs=[out_spec, out_spec],
        out_shape=[jax.ShapeDtypeStruct((t, d), BF16), jax.ShapeDtypeStruct((t, d), BF16)],
        scratch_shapes=[pltpu.VMEM((tm, d), BF16)],
        compiler_params=_cparams(2), name="conv_in",
    )(x, gain.reshape(1, d), w_in, w_in, w_in)


def _conv_out_kernel(gb_ref, z_ref, zprev_ref, cw_ref, w_ref, x_ref, o_ref, *, tiles_per_seq):
    i = pl.program_id(0)
    z = z_ref[...].astype(F32)
    prev = jnp.where(i % tiles_per_seq == 0, 0.0, zprev_ref[...].astype(F32))
    row = lax.broadcasted_iota(jnp.int32, z.shape, 0)
    z1 = jnp.where(row == 0, prev[SUBLANES - 1:SUBLANES], pltpu.roll(z, 1, 0))
    z2 = pltpu.roll(z, 2, 0)
    z2 = jnp.where(row == 0, prev[SUBLANES - 2:SUBLANES - 1], z2)
    z2 = jnp.where(row == 1, prev[SUBLANES - 1:SUBLANES], z2)
    cw = cw_ref[...]
    zc = cw[0:1] * z2 + cw[1:2] * z1 + cw[2:3] * z
    a = (gb_ref[...].astype(F32) * zc).astype(BF16)
    o_ref[...] = x_ref[...] + jnp.dot(a, w_ref[...], preferred_element_type=F32)


def _conv_out(gb, z, conv_w, w_out, layer, x, seq, *, tm=512):
    t, d = x.shape
    cw = jnp.zeros((SUBLANES, d), F32).at[:conv_w.shape[0]].set(conv_w)
    row_spec = pl.BlockSpec((tm, d), lambda i: (i, 0))
    prev_spec = pl.BlockSpec(
        (SUBLANES, d), lambda i: (jnp.maximum(i * (tm // SUBLANES) - 1, 0), 0))
    return pl.pallas_call(
        functools.partial(_conv_out_kernel, tiles_per_seq=seq // tm),
        grid=(t // tm,),
        in_specs=[row_spec, row_spec, prev_spec,
                  pl.BlockSpec((SUBLANES, d), lambda i: (0, 0)),
                  pl.BlockSpec((None, d, d), lambda i: (layer, 0, 0)), row_spec],
        out_specs=row_spec,
        out_shape=jax.ShapeDtypeStruct((t, d), F32),
        compiler_params=_cparams(1), name="conv_out",
    )(gb, z, z, cw, w_out, x)


def _router_kernel(x_ref, g_ref, r_ref, hn_ref, route_ref, gates_ref, counts_ref, carry_ref,
                   *, n_experts):
    i = pl.program_id(0)
    tm = x_ref.shape[0]

    @pl.when(i == 0)
    def _():
        carry_ref[...] = jnp.zeros(carry_ref.shape, F32)

    hn = _rms(x_ref[...], g_ref[...])
    hn_ref[...] = hn
    lane = lax.broadcasted_iota(jnp.int32, (tm, LANES), 1)
    logits = jnp.full((tm, LANES), -jnp.inf, F32)
    for e in range(n_experts):
        logit_e = jnp.sum(hn * r_ref[e:e + 1, :], axis=-1, keepdims=True)
        logits = jnp.where(lane == e, logit_e, logits)
    m1 = jnp.max(logits, axis=-1, keepdims=True)
    i1 = jnp.min(jnp.where(logits == m1, lane, LANES), axis=-1, keepdims=True)
    rest = jnp.where(lane == i1, -jnp.inf, logits)
    m2 = jnp.max(rest, axis=-1, keepdims=True)
    i2 = jnp.min(jnp.where(rest == m2, lane, LANES), axis=-1, keepdims=True)
    e2 = jnp.exp(m2 - m1)
    g1 = 1.0 / (1.0 + e2)
    g2 = e2 / (1.0 + e2)

    oh1 = (lane == i1).astype(F32)
    oh2 = (lane == i2).astype(F32)
    r_i = lax.broadcasted_iota(jnp.int32, (tm, tm), 0)
    c_i = lax.broadcasted_iota(jnp.int32, (tm, tm), 1)
    lower = (c_i < r_i).astype(BF16)
    before1 = jnp.dot(lower, oh1.astype(BF16), preferred_element_type=F32)
    before2 = jnp.dot(lower, oh2.astype(BF16), preferred_element_type=F32)
    tot1 = jnp.sum(oh1, axis=0, keepdims=True)
    tot2 = jnp.sum(oh2, axis=0, keepdims=True)
    carry = carry_ref[0:1]
    rank1 = jnp.sum(oh1 * (carry + before1), axis=-1, keepdims=True)
    rank2 = jnp.sum(oh2 * (carry + tot1 + before2), axis=-1, keepdims=True)
    new_carry = carry + tot1 + tot2
    carry_ref[...] = jnp.broadcast_to(new_carry, carry_ref.shape)
    counts_ref[...] = jnp.broadcast_to(new_carry, counts_ref.shape).astype(jnp.int32)

    route = jnp.where(lane == 0, i1, 0) + jnp.where(lane == 1, i2, 0)
    route += jnp.where(lane == 2, rank1.astype(jnp.int32), 0)
    route += jnp.where(lane == 3, rank2.astype(jnp.int32), 0)
    route_ref[...] = route
    gates_ref[...] = jnp.where(lane == 0, g1, 0.0) + jnp.where(lane == 1, g2, 0.0)


def _router(x, gain, router_w, *, tm=512):
    t, d = x.shape
    n_experts = router_w.shape[1]
    r_t = router_w.T
    row_spec = pl.BlockSpec((tm, d), lambda i: (i, 0))
    lane_spec = pl.BlockSpec((tm, LANES), lambda i: (i, 0))
    return pl.pallas_call(
        functools.partial(_router_kernel, n_experts=n_experts),
        grid=(t // tm,),
        in_specs=[row_spec, pl.BlockSpec((1, d), lambda i: (0, 0)),
                  pl.BlockSpec((n_experts, d), lambda i: (0, 0))],
        out_specs=[row_spec, lane_spec, lane_spec,
                   pl.BlockSpec((SUBLANES, LANES), lambda i: (0, 0))],
        out_shape=[jax.ShapeDtypeStruct((t, d), F32),
                   jax.ShapeDtypeStruct((t, LANES), jnp.int32),
                   jax.ShapeDtypeStruct((t, LANES), F32),
                   jax.ShapeDtypeStruct((SUBLANES, LANES), jnp.int32)],
        scratch_shapes=[pltpu.VMEM((SUBLANES, LANES), F32)],
        compiler_params=_cparams(1), name="moe_router",
    )(x, gain.reshape(1, d), r_t)


def _dispatch_kernel(pos1_ref, pos2_ref, hn_ref, xs_in_ref, xs_ref, sem):
    del xs_in_ref
    tm = hn_ref.shape[0]
    base = pl.program_id(0) * tm

    def row_copy(r, pos_ref):
        return pltpu.make_async_copy(
            hn_ref.at[pl.ds(r, 1)], xs_ref.at[pl.ds(pos_ref[base + r], 1)], sem)

    def issue(r, carry):
        row_copy(r, pos1_ref).start()
        row_copy(r, pos2_ref).start()
        return carry

    def drain(r, carry):
        row_copy(r, pos1_ref).wait()
        row_copy(r, pos2_ref).wait()
        return carry

    lax.fori_loop(0, tm, issue, 0, unroll=8)
    lax.fori_loop(0, tm, drain, 0, unroll=8)


def _dispatch(hn, pos1, pos2, xs0, *, tm=256):
    t, d = hn.shape
    n_rows = xs0.shape[0]
    grid_spec = pltpu.PrefetchScalarGridSpec(
        num_scalar_prefetch=2, grid=(t // tm,),
        in_specs=[pl.BlockSpec((tm, d), lambda i, p1, p2: (i, 0)),
                  pl.BlockSpec(memory_space=pl.ANY)],
        out_specs=pl.BlockSpec(memory_space=pl.ANY),
        scratch_shapes=[pltpu.SemaphoreType.DMA(())])
    return pl.pallas_call(
        _dispatch_kernel, grid_spec=grid_spec,
        out_shape=jax.ShapeDtypeStruct((n_rows, d), hn.dtype),
        input_output_aliases={3: 0},
        compiler_params=_cparams(1), name="moe_dispatch",
    )(pos1, pos2, hn, xs0)


EXPERT_ROW_STEP = 64


def _expert_kernel(te_ref, rows_ref, xs_ref, wg_ref, wu_ref, wd_ref, o_ref,
                   xstage_ref, xb_ref, sem):
    del te_ref
    i, f = pl.program_id(0), pl.program_id(1)
    n_tiles = pl.num_programs(0)
    tm = xb_ref.shape[0]
    rows = rows_ref[i]
    next_rows = rows_ref[jnp.minimum(i + 1, n_tiles - 1)]

    def tile_copy(tile):
        start = pl.multiple_of(tile * tm, SUBLANES)
        return pltpu.make_async_copy(xs_ref.at[pl.ds(start, tm)], xstage_ref, sem)

    @pl.when((f == 0) & (rows > 0))
    def _():
        @pl.when(i == 0)
        def _():
            tile_copy(0).start()
        tile_copy(i).wait()
        xb_ref[...] = xstage_ref[...].astype(BF16)

    @pl.when((f == 1) & (i + 1 < n_tiles) & (rows > 0) & (next_rows > 0))
    def _():
        tile_copy(i + 1).start()

    @pl.when(f == 0)
    def _():
        o_ref[...] = jnp.zeros(o_ref.shape, F32)

    for m in range(EXPERT_ROW_STEP, tm + 1, EXPERT_ROW_STEP):
        @pl.when(rows == m)
        def _():
            o_ref[0:m, :] += _swiglu_partial(xb_ref[0:m, :], wg_ref, wu_ref, wd_ref)


def _experts(xs, tile_expert, tile_rows, wg, wu, wd, layer, *, tm, tf=256):
    n_rows, d = xs.shape
    fe = wg.shape[3]
    nf = fe // tf
    assert nf >= 2
    f_eff = lambda i, f, rows: jnp.where(rows[i] > 0, f, nf - 1)
    grid_spec = pltpu.PrefetchScalarGridSpec(
        num_scalar_prefetch=2, grid=(n_rows // tm, nf),
        in_specs=[
            pl.BlockSpec(memory_space=pl.ANY),
            pl.BlockSpec((None, None, d, tf),
                         lambda i, f, te, rows: (layer, te[i], 0, f_eff(i, f, rows))),
            pl.BlockSpec((None, None, d, tf),
                         lambda i, f, te, rows: (layer, te[i], 0, f_eff(i, f, rows))),
            pl.BlockSpec((None, None, tf, d),
                         lambda i, f, te, rows: (layer, te[i], f_eff(i, f, rows), 0)),
        ],
        out_specs=pl.BlockSpec((tm, d), lambda i, f, te, rows: (i, 0)),
        scratch_shapes=[pltpu.VMEM((tm, d), F32), pltpu.VMEM((tm, d), BF16),
                        pltpu.SemaphoreType.DMA(())])
    return pl.pallas_call(
        _expert_kernel, grid_spec=grid_spec,
        out_shape=jax.ShapeDtypeStruct((n_rows, d), F32),
        compiler_params=_cparams(2), name="moe_experts",
    )(tile_expert, tile_rows, xs, wg, wu, wd)


def _combine_kernel(pos1_ref, pos2_ref, x_ref, gates_ref, ys_ref, o_ref, buf_ref, sems):
    tm = x_ref.shape[0]
    i = pl.program_id(0)
    n_steps = pl.num_programs(0)

    def row_copy(step, r, k, pos_ref):
        slot = step % 2
        return pltpu.make_async_copy(
            ys_ref.at[pl.ds(pos_ref[step * tm + r], 1)],
            buf_ref.at[slot, k, pl.ds(r, 1)], sems.at[slot])

    def issue_step(step):
        def issue(r, carry):
            row_copy(step, r, 0, pos1_ref).start()
            row_copy(step, r, 1, pos2_ref).start()
            return carry
        lax.fori_loop(0, tm, issue, 0, unroll=8)

    def drain_step(step):
        def drain(r, carry):
            row_copy(step, r, 0, pos1_ref).wait()
            row_copy(step, r, 1, pos2_ref).wait()
            return carry
        lax.fori_loop(0, tm, drain, 0, unroll=8)

    @pl.when(i == 0)
    def _():
        issue_step(0)

    @pl.when(i + 1 < n_steps)
    def _():
        issue_step(i + 1)

    drain_step(i)
    gates = gates_ref[...]
    slot = i % 2
    o_ref[...] = (x_ref[...] + gates[:, 0:1] * buf_ref[slot, 0]
                  + gates[:, 1:2] * buf_ref[slot, 1])


def _combine(x, gates, ys, pos1, pos2, *, tm=256):
    t, d = x.shape
    grid_spec = pltpu.PrefetchScalarGridSpec(
        num_scalar_prefetch=2, grid=(t // tm,),
        in_specs=[pl.BlockSpec((tm, d), lambda i, p1, p2: (i, 0)),
                  pl.BlockSpec((tm, LANES), lambda i, p1, p2: (i, 0)),
                  pl.BlockSpec(memory_space=pl.ANY)],
        out_specs=pl.BlockSpec((tm, d), lambda i, p1, p2: (i, 0)),
        scratch_shapes=[pltpu.VMEM((2, TOP_K, tm, d), F32), pltpu.SemaphoreType.DMA((2,))])
    return pl.pallas_call(
        _combine_kernel, grid_spec=grid_spec,
        out_shape=jax.ShapeDtypeStruct((t, d), F32),
        compiler_params=_cparams(1), name="moe_combine",
    )(pos1, pos2, x, gates, ys)


MOE_ROW_TILE = 1152


def _moe_rows(t, n_experts, tm_e=MOE_ROW_TILE):
    return ((t * TOP_K) // tm_e + n_experts) * tm_e


def _moe_layer(x, gain, router_w, wg, wu, wd, layer, xs_buffer, *, tm_e=MOE_ROW_TILE):
    t, d = x.shape
    n_experts = router_w.shape[1]
    hn, route, gates, counts = _router(x, gain, router_w)
    counts = counts[0, :n_experts]
    padded = (counts + tm_e - 1) // tm_e * tm_e
    ends = jnp.cumsum(padded)
    starts = ends - padded
    pos1 = starts[route[:, 0]] + route[:, 2]
    pos2 = starts[route[:, 1]] + route[:, 3]
    n_tiles = xs_buffer.shape[0] // tm_e
    n_valid = ends[-1] // tm_e
    tiles = jnp.arange(n_tiles, dtype=jnp.int32)
    tile_idx = jnp.minimum(tiles, jnp.maximum(n_valid - 1, 0))
    tile_expert = jnp.sum(tile_idx[:, None] * tm_e >= ends[None, :], axis=1).astype(jnp.int32)
    real_rows = jnp.clip(counts[tile_expert] - (tile_idx * tm_e - starts[tile_expert]), 0, tm_e)
    step = EXPERT_ROW_STEP
    tile_rows = jnp.where(tiles < n_valid, (real_rows + step - 1) // step * step, 0)
    xs = _dispatch(hn, pos1, pos2, xs_buffer)
    ys = _experts(xs, tile_expert, tile_rows.astype(jnp.int32), wg, wu, wd, layer, tm=tm_e)
    return _combine(x, gates, ys, pos1, pos2), xs


def kernel(x, attn_norm, attn_w_qkv, attn_q_gain, attn_k_gain, attn_lambda_q1, attn_lambda_k1,
           attn_lambda_q2, attn_lambda_k2, attn_subln_gain, attn_w_o,
           dense_norm, dense_w_gate, dense_w_up, dense_w_down,
           conv_norm, conv_w_in, conv_w, conv_w_out,
           moe_norm, moe_router, moe_w_gate, moe_w_up, moe_w_down):
    batch, seq, d = x.shape
    depth = attn_norm.shape[0] + conv_norm.shape[0]
    tables = _rope_tables(seq)
    h = x.reshape(batch * seq, d)
    w_o, w_out = attn_w_o.astype(BF16), conv_w_out.astype(BF16)
    xs_buffer = jnp.zeros((_moe_rows(batch * seq, moe_router.shape[2]), d), F32)
    for i in range(depth):
        j = i // 2
        if i % 2 == 0:
            lambda_init = 0.8 - 0.6 * math.exp(-0.3 * i)
            lam_params = jnp.zeros((SUBLANES, HEAD_DIM), F32).at[:4].set(jnp.stack(
                [attn_lambda_q1[j], attn_lambda_k1[j], attn_lambda_q2[j], attn_lambda_k2[j]]))
            qkv = _qkv_proj(h, attn_norm[j], attn_w_qkv, j)
            a = _diff_attention(qkv, lam_params, attn_subln_gain[j], attn_q_gain[j],
                                attn_k_gain[j], tables, batch, seq, lambda_init)
            h = _proj_residual(a, w_o, j, h)
            h = _dense_swiglu(h, dense_norm[j], dense_w_gate, dense_w_up, dense_w_down, j)
        else:
            gb, z = _conv_in(h, conv_norm[j], conv_w_in, j)
            h = _conv_out(gb, z, conv_w[j], w_out, j, h, seq)
            h, xs_buffer = _moe_layer(h, moe_norm[j], moe_router[j], moe_w_gate, moe_w_up,
                                      moe_w_down, j, xs_buffer)
    return h.reshape(batch, seq, d)
```
